```python
import math
import jax, jax.numpy as jnp
from jax import lax
import numpy as np

D_MODEL = 1024
BATCH = 4
SEQ = 8192
DEPTH = 1

CHUNK = 64

D_MIX = D_MODEL
D_POOL = D_MIX // 2
D_SSM = D_MIX - D_POOL
POOL_WINDOWS = (2, 4, 8, 16)
N_POOL_GROUPS = len(POOL_WINDOWS)
POOL_GROUP = D_POOL // N_POOL_GROUPS
SSM_GROUP = 16
N_SSM_GROUPS = D_SSM // SSM_GROUP
SSM_STATE = 64
DT_MIN = 1e-3
DT_MAX = 1e-1

N_EXPERT_GROUPS = 4
EXPERTS_PER_GROUP = 4
N_EXPERTS = N_EXPERT_GROUPS * EXPERTS_PER_GROUP
TOP_K_INNER = 2
D_EXPERT = D_MODEL // 4

EPS = 1e-6

kernel_name = "hymba_pool_s5_hier_moe_block"


def rms_norm(x, g):
    xf = x.astype(jnp.float32)
    y = xf * lax.rsqrt(jnp.mean(xf * xf, axis=-1, keepdims=True) + EPS)
    return (y * g.astype(jnp.float32)).astype(x.dtype)


def pool_mixer(u, pool_w, pool_scale):
    bsz, l, _ = u.shape
    uf = u.astype(jnp.float32)
    cs = jnp.pad(jnp.cumsum(uf, axis=1), ((0, 0), (1, 0), (0, 0)))
    pos = jnp.arange(l)
    outs = []
    for gi, w in enumerate(POOL_WINDOWS):
        c = cs[:, :, gi * POOL_GROUP:(gi + 1) * POOL_GROUP]
        start = jnp.maximum(pos + 1 - w, 0)
        win_sum = c[:, 1:] - c[:, start]
        count = jnp.minimum(pos + 1, w).astype(jnp.float32)
        outs.append(win_sum / count[None, :, None] - uf[:, :, gi * POOL_GROUP:(gi + 1) * POOL_GROUP])
    p = jnp.stack(outs, axis=2)
    p = jnp.einsum('blgc,gcd->blgd', p, pool_w.astype(jnp.float32))
    return (p.reshape(bsz, l, D_POOL) * pool_scale.astype(jnp.float32)).astype(u.dtype)


def _complex_scan_combine(e1, e2):
    a1r, a1i, b1r, b1i = e1
    a2r, a2i, b2r, b2i = e2
    ar = a2r * a1r - a2i * a1i
    ai = a2r * a1i + a2i * a1r
    br = a2r * b1r - a2i * b1i + b2r
    bi = a2r * b1i + a2i * b1r + b2i
    return ar, ai, br, bi


def s5_mixer(u, a_re, a_im, log_step, b_re, b_im, c_re, c_im, d_skip, glu_w, glu_b):
    f32 = jnp.float32
    bsz, l, _ = u.shape
    uf = u.astype(f32).reshape(bsz, l, N_SSM_GROUPS, SSM_GROUP)
    lr = a_re.astype(f32)
    li = a_im.astype(f32)
    step = jnp.exp(log_step.astype(f32))[:, None]
    mag = jnp.exp(lr * step)
    ab_re = mag * jnp.cos(li * step)
    ab_im = mag * jnp.sin(li * step)
    den = lr * lr + li * li
    nr = ab_re - 1.0
    ni = ab_im
    q_re = (nr * lr + ni * li) / den
    q_im = (ni * lr - nr * li) / den
    br = b_re.astype(f32)
    bi = b_im.astype(f32)
    bb_re = q_re[..., None] * br - q_im[..., None] * bi
    bb_im = q_re[..., None] * bi + q_im[..., None] * br
    bu_re = jnp.einsum('blgh,gph->blgp', uf, bb_re)
    bu_im = jnp.einsum('blgh,gph->blgp', uf, bb_im)
    a_seq_re = jnp.broadcast_to(ab_re, (1, l) + ab_re.shape)
    a_seq_im = jnp.broadcast_to(ab_im, (1, l) + ab_im.shape)
    _, _, x_re, x_im = lax.associative_scan(
        _complex_scan_combine, (a_seq_re, a_seq_im, bu_re, bu_im), axis=1)
    y = (jnp.einsum('blgp,ghp->blgh', x_re, c_re.astype(f32))
         - jnp.einsum('blgp,ghp->blgh', x_im, c_im.astype(f32))
         + d_skip.astype(f32).reshape(N_SSM_GROUPS, SSM_GROUP) * uf)
    y = jax.nn.gelu(y.reshape(bsz, l, D_SSM))
    y = y * jax.nn.sigmoid(y @ glu_w.astype(f32) + glu_b.astype(f32))
    return y.astype(u.dtype)


def hier_moe(h, w_coarse, b_coarse, w_fine, b_fine, w_gate, w_up, w_down):
    f32 = jnp.float32
    bsz, l, d = h.shape
    t = h.reshape(-1, d)
    coarse = (t @ w_coarse).astype(f32) + b_coarse.astype(f32)
    p_coarse = jax.nn.softmax(coarse, axis=-1)
    g_idx = jnp.argmax(coarse, axis=-1)
    p_g = jnp.take_along_axis(p_coarse, g_idx[:, None], axis=-1)
    fine = ((t @ w_fine).astype(f32) + b_fine.astype(f32)).reshape(-1, N_EXPERT_GROUPS, EXPERTS_PER_GROUP)
    fine_sel = jnp.take_along_axis(fine, g_idx[:, None, None], axis=1)[:, 0]
    top_v, top_i = lax.top_k(fine_sel, TOP_K_INNER)
    w_sel = jax.nn.softmax(top_v, axis=-1) * p_g
    e_idx = g_idx[:, None] * EXPERTS_PER_GROUP + top_i
    gates = jnp.sum(jax.nn.one_hot(e_idx, N_EXPERTS, dtype=f32) * w_sel[..., None], axis=1)
    out = jnp.zeros(t.shape, f32)
    for e in range(N_EXPERTS):
        a = jax.nn.silu(t @ w_gate[e]) * (t @ w_up[e])
        out = out + gates[:, e:e + 1] * (a @ w_down[e]).astype(f32)
    return out.reshape(bsz, l, d).astype(h.dtype)


def setup_inputs(seed: int = 0) -> dict:
    key = jax.random.key(seed)
    ks = jax.random.split(key, 26)
    f32 = jnp.float32
    nrm = lambda k, shape, s: (jax.random.normal(k, shape, f32) * s).astype(f32)
    x = jax.random.normal(ks[0], (BATCH, SEQ, D_MODEL), f32)
    norm_mix = 1.0 + nrm(ks[1], (DEPTH, D_MODEL), 0.02)
    w_in = nrm(ks[2], (DEPTH, D_MODEL, D_MIX), D_MODEL ** -0.5)
    pool_w = nrm(ks[3], (DEPTH, N_POOL_GROUPS, POOL_GROUP, POOL_GROUP), POOL_GROUP ** -0.5)
    pool_scale = 1.0 + nrm(ks[4], (DEPTH, D_POOL), 0.02)
    n = jnp.arange(SSM_STATE, dtype=f32)
    ssm_a_re = -0.5 + nrm(ks[5], (DEPTH, N_SSM_GROUPS, SSM_STATE), 0.01)
    ssm_a_im = math.pi * n[None, None, :] + nrm(ks[6], (DEPTH, N_SSM_GROUPS, SSM_STATE), 0.01)
    ssm_log_step = jax.random.uniform(ks[7], (DEPTH, N_SSM_GROUPS), f32,
                                      math.log(DT_MIN), math.log(DT_MAX))
    ssm_b_re = nrm(ks[8], (DEPTH, N_SSM_GROUPS, SSM_STATE, SSM_GROUP), (2 * SSM_GROUP) ** -0.5)
    ssm_b_im = nrm(ks[9], (DEPTH, N_SSM_GROUPS, SSM_STATE, SSM_GROUP), (2 * SSM_GROUP) ** -0.5)
    ssm_c_re = nrm(ks[10], (DEPTH, N_SSM_GROUPS, SSM_GROUP, SSM_STATE), (2 * SSM_STATE) ** -0.5)
    ssm_c_im = nrm(ks[11], (DEPTH, N_SSM_GROUPS, SSM_GROUP, SSM_STATE), (2 * SSM_STATE) ** -0.5)
    ssm_d = 1.0 + nrm(ks[12], (DEPTH, D_SSM), 0.1)
    glu_w = nrm(ks[13], (DEPTH, D_SSM, D_SSM), D_SSM ** -0.5)
    glu_b = nrm(ks[14], (DEPTH, D_SSM), 0.02)
    w_out = nrm(ks[15], (DEPTH, D_MIX, D_MODEL), D_MIX ** -0.5)
    norm_ffn = 1.0 + nrm(ks[16], (DEPTH, D_MODEL), 0.02)
    router_coarse_w = nrm(ks[17], (DEPTH, D_MODEL, N_EXPERT_GROUPS), D_MODEL ** -0.5)
    router_coarse_b = nrm(ks[18], (DEPTH, N_EXPERT_GROUPS), 0.01)
    router_fine_w = nrm(ks[19], (DEPTH, D_MODEL, N_EXPERTS), D_MODEL ** -0.5)
    router_fine_b = nrm(ks[20], (DEPTH, N_EXPERTS), 0.01)
    exp_w_gate = nrm(ks[21], (DEPTH, N_EXPERTS, D_MODEL, D_EXPERT), D_MODEL ** -0.5)
    exp_w_up = nrm(ks[22], (DEPTH, N_EXPERTS, D_MODEL, D_EXPERT), D_MODEL ** -0.5)
    exp_w_down = nrm(ks[23], (DEPTH, N_EXPERTS, D_EXPERT, D_MODEL), D_EXPERT ** -0.5)
    norm_final = 1.0 + nrm(ks[24], (D_MODEL,), 0.02)
    return {"x": x, "norm_mix": norm_mix, "w_in": w_in, "pool_w": pool_w, "pool_scale": pool_scale,
            "ssm_a_re": ssm_a_re, "ssm_a_im": ssm_a_im, "ssm_log_step": ssm_log_step,
            "ssm_b_re": ssm_b_re, "ssm_b_im": ssm_b_im, "ssm_c_re": ssm_c_re, "ssm_c_im": ssm_c_im,
            "ssm_d": ssm_d, "glu_w": glu_w, "glu_b": glu_b, "w_out": w_out, "norm_ffn": norm_ffn,
            "router_coarse_w": router_coarse_w, "router_coarse_b": router_coarse_b,
            "router_fine_w": router_fine_w, "router_fine_b": router_fine_b,
            "exp_w_gate": exp_w_gate, "exp_w_up": exp_w_up, "exp_w_down": exp_w_down,
            "norm_final": norm_final}


def reference(x, norm_mix, w_in, pool_w, pool_scale, ssm_a_re, ssm_a_im, ssm_log_step,
              ssm_b_re, ssm_b_im, ssm_c_re, ssm_c_im, ssm_d, glu_w, glu_b, w_out, norm_ffn,
              router_coarse_w, router_coarse_b, router_fine_w, router_fine_b,
              exp_w_gate, exp_w_up, exp_w_down, norm_final):
    for i in range(DEPTH):
        hn = rms_norm(x, norm_mix[i])
        z = hn @ w_in[i]
        y_pool = pool_mixer(z[..., :D_POOL], pool_w[i], pool_scale[i])
        y_ssm = s5_mixer(z[..., D_POOL:], ssm_a_re[i], ssm_a_im[i], ssm_log_step[i],
                         ssm_b_re[i], ssm_b_im[i], ssm_c_re[i], ssm_c_im[i], ssm_d[i],
                         glu_w[i], glu_b[i])
        x = x + jnp.concatenate([y_pool, y_ssm], axis=-1) @ w_out[i]
        x = x + hier_moe(rms_norm(x, norm_ffn[i]), router_coarse_w[i], router_coarse_b[i],
                         router_fine_w[i], router_fine_b[i],
                         exp_w_gate[i], exp_w_up[i], exp_w_down[i])
    return rms_norm(x, norm_final)
```

```python
import functools
import math

import jax
import jax.numpy as jnp
from jax import lax
from jax.experimental import pallas as pl
from jax.experimental.pallas import tpu as pltpu

EPS = 1e-6
POOL_WINDOWS = (2, 4, 8, 16)
POOL_GROUP = 128
POOL_TAIL = 16
SSM_GROUP = 16
SSM_STATE = 64
N_EXPERT_GROUPS = 4
EXPERTS_PER_GROUP = 4
N_EXPERTS = 16

SUBLANES = 8
LANES = 128
SLAB_GROUPS = 8
STATE_SLAB = SLAB_GROUPS * SSM_STATE
TIME_TILE = 256
MOE_TILE = 1024
ROUTER_LANES = 128
VMEM_LIMIT = 56 * 1024 * 1024

F32 = jnp.float32
BF16 = jnp.bfloat16


def _dot(a, b):
    return jnp.dot(a, b, preferred_element_type=F32)


def _rms(x, g):
    return x * lax.rsqrt(jnp.mean(x * x, axis=-1, keepdims=True) + EPS) * g


def _mixer_kernel(x_ref, nm_ref, win_ref, pw_ref, ps_ref, perm_ref, permt_ref, wb_ref,
                  are_ref, aim_ref, apre_ref, apim_ref, mkre_ref, mkim_ref, asre_ref, asim_ref,
                  wcre_ref, wcim_ref, d_ref, gw_ref, gb_ref, wout_ref,
                  o_ref, zpad, bur, bui, st_r, st_i, *, tc, d_pool, d_ssm):
    n = pl.program_id(1)
    seg = tc // SUBLANES
    n_slab = d_ssm // LANES

    @pl.when(n == 0)
    def _():
        zpad[0:POOL_TAIL, :] = jnp.zeros((POOL_TAIL, d_pool), F32)
        st_r[...] = jnp.zeros_like(st_r)
        st_i[...] = jnp.zeros_like(st_i)

    x = x_ref[0]
    hn = _rms(x, nm_ref[...]).astype(BF16)
    z = _dot(hn, win_ref[...])

    zp = z[:, :d_pool]
    zpad[POOL_TAIL:POOL_TAIL + tc, :] = zp
    t = n * tc + lax.broadcasted_iota(jnp.int32, (tc, 1), 0)
    pooled = []
    for gi, w in enumerate(POOL_WINDOWS):
        cols = slice(gi * POOL_GROUP, (gi + 1) * POOL_GROUP)
        acc = zp[:, cols]
        for k in range(1, w):
            acc = acc + zpad[POOL_TAIL - k:POOL_TAIL - k + tc, cols]
        cnt = jnp.minimum(t + 1, w).astype(F32)
        p = acc / cnt - zp[:, cols]
        pooled.append(_dot(p.astype(BF16), pw_ref[gi]))
    y_pool = jnp.concatenate(pooled, axis=-1) * ps_ref[...]
    zpad[0:POOL_TAIL, :] = zpad[tc:tc + POOL_TAIL, :]

    u = z[:, d_pool:]
    up = _dot(perm_ref[...], u.astype(BF16))
    upb = up.astype(BF16)
    for m in range(n_slab):
        r = _dot(upb[:, m * LANES:(m + 1) * LANES], wb_ref[m])
        bur[:, m * STATE_SLAB:(m + 1) * STATE_SLAB] = r[:, :STATE_SLAB]
        bui[:, m * STATE_SLAB:(m + 1) * STATE_SLAB] = r[:, STATE_SLAB:]

    row = lax.broadcasted_iota(jnp.int32, (SUBLANES, STATE_SLAB), 0)
    for c in range(n_slab):
        cs = slice(c * STATE_SLAB, (c + 1) * STATE_SLAB)
        ar = are_ref[:, cs]
        ai = aim_ref[:, cs]

        def scan_body(s, carry, cs=cs, ar=ar, ai=ai):
            xr, xi = carry
            r0 = pl.multiple_of(s * SUBLANES, SUBLANES)
            nxr = ar * xr - ai * xi + bur[pl.ds(r0, SUBLANES), cs]
            nxi = ar * xi + ai * xr + bui[pl.ds(r0, SUBLANES), cs]
            bur[pl.ds(r0, SUBLANES), cs] = nxr
            bui[pl.ds(r0, SUBLANES), cs] = nxi
            return nxr, nxi

        zero = jnp.zeros((SUBLANES, STATE_SLAB), F32)
        er, ei = lax.fori_loop(0, seg, scan_body, (zero, zero), unroll=4)

        fr = jnp.where(row == 0, st_r[:, cs], pltpu.roll(er, 1, 0))
        fi = jnp.where(row == 0, st_i[:, cs], pltpu.roll(ei, 1, 0))
        for ki, k in enumerate((1, 2, 4)):
            mr = mkre_ref[ki, :, cs]
            mi = mkim_ref[ki, :, cs]
            rr = pltpu.roll(fr, k, 0)
            ri = pltpu.roll(fi, k, 0)
            fr, fi = fr + mr * rr - mi * ri, fi + mr * ri + mi * rr
        asr = asre_ref[:, cs]
        asi = asim_ref[:, cs]
        nsr = asr * fr - asi * fi + er
        nsi = asr * fi + asi * fr + ei
        st_r[:, cs] = pltpu.roll(nsr, 1, 0)
        st_i[:, cs] = pltpu.roll(nsi, 1, 0)

        def fix_body(s, carry, cs=cs, fr=fr, fi=fi):
            r0 = pl.multiple_of(s * SUBLANES, SUBLANES)
            pr = apre_ref[pl.ds(r0, SUBLANES), cs]
            pi = apim_ref[pl.ds(r0, SUBLANES), cs]
            bur[pl.ds(r0, SUBLANES), cs] = bur[pl.ds(r0, SUBLANES), cs] + pr * fr - pi * fi
            bui[pl.ds(r0, SUBLANES), cs] = bui[pl.ds(r0, SUBLANES), cs] + pr * fi + pi * fr
            return carry

        lax.fori_loop(0, seg, fix_body, 0, unroll=4)

    ys = []
    for m in range(n_slab):
        cs = slice(m * STATE_SLAB, (m + 1) * STATE_SLAB)
        ys.append(_dot(bur[:, cs].astype(BF16), wcre_ref[m]) + _dot(bui[:, cs].astype(BF16), wcim_ref[m]))
    y = jnp.concatenate(ys, axis=-1) + d_ref[...] * up
    y = jax.nn.gelu(y)
    y = y * jax.nn.sigmoid(_dot(y.astype(BF16), gw_ref[...]) + gb_ref[...])
    y_ssm = _dot(permt_ref[...], y.astype(BF16))

    mix = _dot(y_pool.astype(BF16), wout_ref[0:d_pool, :]) + _dot(y_ssm.astype(BF16), wout_ref[d_pool:, :])
    o_ref[0] = x + mix


def _ssm_tables(a_re, a_im, log_step, b_re, b_im, c_re, c_im, seg):
    g, p = a_re.shape
    h = b_re.shape[-1]
    n_slab = g // SLAB_GROUPS
    lr, li = a_re.astype(F32), a_im.astype(F32)
    step = jnp.exp(log_step.astype(F32))[:, None]
    mag = jnp.exp(lr * step)
    ab_re = mag * jnp.cos(li * step)
    ab_im = mag * jnp.sin(li * step)
    den = lr * lr + li * li
    nr, ni = ab_re - 1.0, ab_im
    q_re = (nr * lr + ni * li) / den
    q_im = (ni * lr - nr * li) / den
    br, bi = b_re.astype(F32), b_im.astype(F32)
    bb_re = q_re[..., None] * br - q_im[..., None] * bi
    bb_im = q_re[..., None] * bi + q_im[..., None] * br

    eye = jnp.eye(SLAB_GROUPS, dtype=F32)

    def b_slab(bb):
        bb = bb.reshape(n_slab, SLAB_GROUPS, p, h)
        w = jnp.einsum('mgph,gk->mghkp', bb, eye)
        return w.reshape(n_slab, SLAB_GROUPS * h, SLAB_GROUPS * p)

    wb = jnp.concatenate([b_slab(bb_re), b_slab(bb_im)], axis=-1).astype(BF16)

    def c_slab(cc):
        cc = cc.astype(F32).reshape(n_slab, SLAB_GROUPS, h, p)
        w = jnp.einsum('mghp,gk->mgpkh', cc, eye)
        return w.reshape(n_slab, SLAB_GROUPS * p, SLAB_GROUPS * h)

    wc_re = c_slab(c_re).astype(BF16)
    wc_im = c_slab(-c_im).astype(BF16)

    def apow(nn):
        nn = nn.astype(F32)[:, None, None]
        m_ = jnp.exp(nn * (lr * step))
        ang = nn * (li * step)
        return (m_ * jnp.cos(ang)).reshape(-1, g * p), (m_ * jnp.sin(ang)).reshape(-1, g * p)

    a1r, a1i = apow(jnp.ones((1,), F32))
    a_r = jnp.broadcast_to(a1r, (SUBLANES, g * p))
    a_i = jnp.broadcast_to(a1i, (SUBLANES, g * p))
    pr, pi = apow(jnp.arange(1, seg + 1))
    ap_r = jnp.repeat(pr, SUBLANES, axis=0)
    ap_i = jnp.repeat(pi, SUBLANES, axis=0)
    ks = jnp.array([1, 2, 4])
    mr, mi = apow(ks * seg)
    rows = jnp.arange(SUBLANES)[None, :, None]
    mask = (rows >= ks[:, None, None]).astype(F32)
    mk_r = mr[:, None, :] * mask
    mk_i = mi[:, None, :] * mask
    sr, si = apow(jnp.full((1,), seg, F32))
    as_r = jnp.broadcast_to(sr, (SUBLANES, g * p))
    as_i = jnp.broadcast_to(si, (SUBLANES, g * p))
    return wb, a_r, a_i, ap_r, ap_i, mk_r, mk_i, as_r, as_i, wc_re, wc_im


def _mixer(x, norm_mix, w_in, pool_w, pool_scale, a_re, a_im, log_step, b_re, b_im, c_re, c_im,
           d_skip, glu_w, glu_b, w_out):
    b, l, d = x.shape
    d_pool = pool_scale.shape[-1]
    d_ssm = d_skip.shape[-1]
    tc = TIME_TILE
    assert l % tc == 0 and d_pool == len(POOL_WINDOWS) * POOL_GROUP and d_ssm % LANES == 0
    seg = tc // SUBLANES
    n_state = a_re.shape[0] * a_re.shape[1]
    tables = _ssm_tables(a_re, a_im, log_step, b_re, b_im, c_re, c_im, seg)
    wb, a_r, a_i, ap_r, ap_i, mk_r, mk_i, as_r, as_i, wc_re, wc_im = tables

    i = jnp.arange(tc)
    src = (i % SUBLANES) * seg + i // SUBLANES
    perm = (src[:, None] == jnp.arange(tc)[None, :]).astype(BF16)
    permt = perm.T

    row2 = lambda a: a.reshape(1, -1).astype(F32)
    operands = [
        x, row2(norm_mix), w_in.astype(BF16), pool_w.astype(BF16), row2(pool_scale), perm, permt, wb,
        a_r, a_i, ap_r, ap_i, mk_r, mk_i, as_r, as_i, wc_re, wc_im,
        row2(d_skip), glu_w.astype(BF16), row2(glu_b), w_out.astype(BF16),
    ]

    def whole(a):
        nd = a.ndim
        return pl.BlockSpec(a.shape, lambda bi, ni, nd=nd: (0,) * nd)

    in_specs = [pl.BlockSpec((1, tc, d), lambda bi, ni: (bi, ni, 0))] + [whole(a) for a in operands[1:]]
    kern = functools.partial(_mixer_kernel, tc=tc, d_pool=d_pool, d_ssm=d_ssm)
    return pl.pallas_call(
        kern,
        grid=(b, l // tc),
        in_specs=in_specs,
        out_specs=pl.BlockSpec((1, tc, d), lambda bi, ni: (bi, ni, 0)),
        out_shape=jax.ShapeDtypeStruct((b, l, d), F32),
        scratch_shapes=[
            pltpu.VMEM((POOL_TAIL + tc, d_pool), F32),
            pltpu.VMEM((tc, n_state), F32),
            pltpu.VMEM((tc, n_state), F32),
            pltpu.VMEM((SUBLANES, n_state), F32),
            pltpu.VMEM((SUBLANES, n_state), F32),
        ],
        compiler_params=pltpu.CompilerParams(
            dimension_semantics=("arbitrary", "arbitrary"), vmem_limit_bytes=VMEM_LIMIT),
        name="mixer",
    )(*operands)


def _moe_kernel(x_ref, nf_ref, wr_ref, br_ref, wg_ref, wu_ref, wd_ref, nfin_ref,
                o_ref, h_s, gate_s, acc_s):
    e = pl.program_id(1)
    tm = x_ref.shape[0]
    lane = lax.broadcasted_iota(jnp.int32, (tm, ROUTER_LANES), 1)
    n_g = N_EXPERT_GROUPS

    @pl.when(e == 0)
    def _():
        hb = _rms(x_ref[...], nf_ref[...]).astype(BF16)
        h_s[...] = hb
        logits = _dot(hb, wr_ref[...]) + br_ref[...]
        neg = jnp.float32(-3.0e38)
        is_c = lane < n_g
        cm = jnp.where(is_c, logits, neg)
        cmax = jnp.max(cm, axis=-1, keepdims=True)
        gidx = jnp.min(jnp.where(cm == cmax, lane, ROUTER_LANES), axis=-1, keepdims=True)
        psum = jnp.sum(jnp.where(is_c, jnp.exp(logits - cmax), 0.0), axis=-1, keepdims=True)
        p_g = 1.0 / psum
        eid = lane - n_g
        insel = (lane >= n_g) & (lane < n_g + N_EXPERTS) & ((eid // EXPERTS_PER_GROUP) == gidx)
        fs = jnp.where(insel, logits, neg)
        v0 = jnp.max(fs, axis=-1, keepdims=True)
        i0 = jnp.min(jnp.where(insel & (fs == v0), lane, ROUTER_LANES), axis=-1, keepdims=True)
        rest = insel & (lane != i0)
        fs2 = jnp.where(rest, logits, neg)
        v1 = jnp.max(fs2, axis=-1, keepdims=True)
        i1 = jnp.min(jnp.where(rest & (fs2 == v1), lane, ROUTER_LANES), axis=-1, keepdims=True)
        ex = jnp.exp(v1 - v0)
        den = 1.0 + ex
        w0 = p_g * (1.0 / den)
        w1 = p_g * (ex / den)
        gate_s[...] = jnp.where(lane == i0, w0, 0.0) + jnp.where(lane == i1, w1, 0.0)
        acc_s[...] = jnp.zeros_like(acc_s)

    hb = h_s[...]
    a = jax.nn.silu(_dot(hb, wg_ref[0])) * _dot(hb, wu_ref[0])
    ge = jnp.sum(jnp.where(lane == e + n_g, gate_s[...], 0.0), axis=-1, keepdims=True)
    acc_s[...] += _dot((a * ge).astype(BF16), wd_ref[0])

    @pl.when(e == pl.num_programs(1) - 1)
    def _():
        o_ref[...] = _rms(x_ref[...] + acc_s[...], nfin_ref[...])


def _moe(x1, norm_ffn, w_coarse, b_coarse, w_fine, b_fine, w_gate, w_up, w_down, norm_final):
    t, d = x1.shape
    n_e, _, f = w_gate.shape
    tm = MOE_TILE
    assert t % tm == 0 and n_e == N_EXPERTS
    pad = ROUTER_LANES - N_EXPERT_GROUPS - N_EXPERTS
    wr = jnp.concatenate([w_coarse, w_fine, jnp.zeros((d, pad), F32)], axis=-1).astype(BF16)
    br = jnp.concatenate([b_coarse, b_fine, jnp.zeros((pad,), F32)]).reshape(1, -1).astype(F32)
    row2 = lambda a: a.reshape(1, -1).astype(F32)
    const = lambda shape: pl.BlockSpec(shape, lambda i, e: (0,) * len(shape))
    return pl.pallas_call(
        _moe_kernel,
        grid=(t // tm, n_e),
        in_specs=[
            pl.BlockSpec((tm, d), lambda i, e: (i, 0)),
            const((1, d)), const((d, ROUTER_LANES)), const((1, ROUTER_LANES)),
            pl.BlockSpec((1, d, f), lambda i, e: (e, 0, 0)),
            pl.BlockSpec((1, d, f), lambda i, e: (e, 0, 0)),
            pl.BlockSpec((1, f, d), lambda i, e: (e, 0, 0)),
            const((1, d)),
        ],
        out_specs=pl.BlockSpec((tm, d), lambda i, e: (i, 0)),
        out_shape=jax.ShapeDtypeStruct((t, d), F32),
        scratch_shapes=[
            pltpu.VMEM((tm, d), BF16),
            pltpu.VMEM((tm, ROUTER_LANES), F32),
            pltpu.VMEM((tm, d), F32),
        ],
        compiler_params=pltpu.CompilerParams(
            dimension_semantics=("arbitrary", "arbitrary"), vmem_limit_bytes=VMEM_LIMIT),
        name="moe",
    )(x1, row2(norm_ffn), wr, br, w_gate.astype(BF16), w_up.astype(BF16), w_down.astype(BF16),
      row2(norm_final))


def kernel(x, norm_mix, w_in, pool_w, pool_scale, ssm_a_re, ssm_a_im, ssm_log_step, ssm_b_re, ssm_b_im, ssm_c_re, ssm_c_im, ssm_d, glu_w, glu_b, w_out, norm_ffn, router_coarse_w, router_coarse_b, router_fine_w, router_fine_b, exp_w_gate, exp_w_up, exp_w_down, norm_final):
    assert norm_mix.shape[0] == 1, "the moe call fuses the final norm: single-layer blocks only"
    b, l, d = x.shape
    x1 = _mixer(x, norm_mix[0], w_in[0], pool_w[0], pool_scale[0], ssm_a_re[0], ssm_a_im[0],
                ssm_log_step[0], ssm_b_re[0], ssm_b_im[0], ssm_c_re[0], ssm_c_im[0], ssm_d[0],
                glu_w[0], glu_b[0], w_out[0])
    y = _moe(x1.reshape(b * l, d), norm_ffn[0], router_coarse_w[0], router_coarse_b[0],
             router_fine_w[0], router_fine_b[0], exp_w_gate[0], exp_w_up[0], exp_w_down[0], norm_final)
    return y.reshape(b, l, d)
```

```python
import functools

import jax
import jax.numpy as jnp
from jax import lax
from jax.experimental import pallas as pl
from jax.experimental.pallas import tpu as pltpu

EPS = 1e-6
POOL_WINDOWS = (2, 4, 8, 16)
POOL_GROUP = 128
POOL_TAIL = 16
SSM_GROUP = 16
SSM_STATE = 64
N_EXPERT_GROUPS = 4
EXPERTS_PER_GROUP = 4
N_EXPERTS = 16

SUBLANES = 8
LANES = 128
SLAB_GROUPS = 8
STATE_SLAB = SLAB_GROUPS * SSM_STATE
TIME_TILE = 256
TIME_TILE_LOG2 = 8
MOE_ROWS = 256
MOE_ROWS_LOG2 = 8
ROUTER_LANES = 128
VMEM_LIMIT = 56 * 1024 * 1024
NEG = -3.0e38

F32 = jnp.float32
BF16 = jnp.bfloat16
I32 = jnp.int32
NT_DIMS = (((1,), (1,)), ((), ()))


def _dot(a, b):
    return jnp.dot(a, b, preferred_element_type=F32)


def _rms(x, g):
    return x * lax.rsqrt(jnp.mean(x * x, axis=-1, keepdims=True) + EPS) * g


def _mixer_kernel(x_ref, nm_ref, win_ref, pw_ref, ps_ref, perm_ref, permt_ref, wb_ref,
                  are_ref, aim_ref, apre_ref, apim_ref, mkre_ref, mkim_ref, asre_ref, asim_ref,
                  wcre_ref, wcim_ref, d_ref, gw_ref, gb_ref, wout_ref, nf_ref, wrc_ref, brc_ref, tri_ref,
                  o_ref, code_ref, cnt_ref, zpad, bur, bui, st_r, st_i, *, tc, d_pool, d_ssm):
    n = pl.program_id(1)
    seg = tc // SUBLANES
    n_slab = d_ssm // LANES

    @pl.when(n == 0)
    def _():
        zpad[0:POOL_TAIL, :] = jnp.zeros((POOL_TAIL, d_pool), F32)
        st_r[...] = jnp.zeros_like(st_r)
        st_i[...] = jnp.zeros_like(st_i)

    x = x_ref[0]
    hn = _rms(x, nm_ref[...]).astype(BF16)
    z = _dot(hn, win_ref[...])

    zp = z[:, :d_pool]
    zpad[POOL_TAIL:POOL_TAIL + tc, :] = zp
    t = n * tc + lax.broadcasted_iota(I32, (tc, 1), 0)
    pooled = []
    for gi, w in enumerate(POOL_WINDOWS):
        cols = slice(gi * POOL_GROUP, (gi + 1) * POOL_GROUP)
        acc = zp[:, cols]
        for k in range(1, w):
            acc = acc + zpad[POOL_TAIL - k:POOL_TAIL - k + tc, cols]
        cnt = jnp.minimum(t + 1, w).astype(F32)
        p = acc / cnt - zp[:, cols]
        pooled.append(_dot(p.astype(BF16), pw_ref[gi]))
    y_pool = jnp.concatenate(pooled, axis=-1) * ps_ref[...]
    zpad[0:POOL_TAIL, :] = zpad[tc:tc + POOL_TAIL, :]

    u = z[:, d_pool:]
    up = _dot(perm_ref[...], u.astype(BF16))
    upb = up.astype(BF16)
    for m in range(n_slab):
        r = _dot(upb[:, m * LANES:(m + 1) * LANES], wb_ref[m])
        bur[:, m * STATE_SLAB:(m + 1) * STATE_SLAB] = r[:, :STATE_SLAB]
        bui[:, m * STATE_SLAB:(m + 1) * STATE_SLAB] = r[:, STATE_SLAB:]

    row = lax.broadcasted_iota(I32, (SUBLANES, STATE_SLAB), 0)
    for c in range(n_slab):
        cs = slice(c * STATE_SLAB, (c + 1) * STATE_SLAB)
        ar = are_ref[:, cs]
        ai = aim_ref[:, cs]

        def scan_body(s, carry, cs=cs, ar=ar, ai=ai):
            xr, xi = carry
            r0 = pl.multiple_of(s * SUBLANES, SUBLANES)
            nxr = ar * xr - ai * xi + bur[pl.ds(r0, SUBLANES), cs]
            nxi = ar * xi + ai * xr + bui[pl.ds(r0, SUBLANES), cs]
            bur[pl.ds(r0, SUBLANES), cs] = nxr
            bui[pl.ds(r0, SUBLANES), cs] = nxi
            return nxr, nxi

        zero = jnp.zeros((SUBLANES, STATE_SLAB), F32)
        er, ei = lax.fori_loop(0, seg, scan_body, (zero, zero), unroll=4)

        fr = jnp.where(row == 0, st_r[:, cs], pltpu.roll(er, 1, 0))
        fi = jnp.where(row == 0, st_i[:, cs], pltpu.roll(ei, 1, 0))
        for ki, k in enumerate((1, 2, 4)):
            mr = mkre_ref[ki, :, cs]
            mi = mkim_ref[ki, :, cs]
            rr = pltpu.roll(fr, k, 0)
            ri = pltpu.roll(fi, k, 0)
            fr, fi = fr + mr * rr - mi * ri, fi + mr * ri + mi * rr
        asr = asre_ref[:, cs]
        asi = asim_ref[:, cs]
        nsr = asr * fr - asi * fi + er
        nsi = asr * fi + asi * fr + ei
        st_r[:, cs] = pltpu.roll(nsr, 1, 0)
        st_i[:, cs] = pltpu.roll(nsi, 1, 0)

        def fix_body(s, carry, cs=cs, fr=fr, fi=fi):
            r0 = pl.multiple_of(s * SUBLANES, SUBLANES)
            pr = apre_ref[pl.ds(r0, SUBLANES), cs]
            pi = apim_ref[pl.ds(r0, SUBLANES), cs]
            bur[pl.ds(r0, SUBLANES), cs] = bur[pl.ds(r0, SUBLANES), cs] + pr * fr - pi * fi
            bui[pl.ds(r0, SUBLANES), cs] = bui[pl.ds(r0, SUBLANES), cs] + pr * fi + pi * fr
            return carry

        lax.fori_loop(0, seg, fix_body, 0, unroll=4)

    ys = []
    for m in range(n_slab):
        cs = slice(m * STATE_SLAB, (m + 1) * STATE_SLAB)
        ys.append(_dot(bur[:, cs].astype(BF16), wcre_ref[m]) + _dot(bui[:, cs].astype(BF16), wcim_ref[m]))
    y = jnp.concatenate(ys, axis=-1) + d_ref[...] * up
    y = jax.nn.gelu(y)
    y = y * jax.nn.sigmoid(_dot(y.astype(BF16), gw_ref[...]) + gb_ref[...])
    y_ssm = _dot(permt_ref[...], y.astype(BF16))

    mix = _dot(y_pool.astype(BF16), wout_ref[0:d_pool, :]) + _dot(y_ssm.astype(BF16), wout_ref[d_pool:, :])
    x1 = x + mix
    o_ref[0] = x1

    hb = _rms(x1, nf_ref[...]).astype(BF16)
    lg = _dot(hb, wrc_ref[...]) + brc_ref[...]
    lane = lax.broadcasted_iota(I32, (tc, ROUTER_LANES), 1)
    cm = jnp.where(lane < N_EXPERT_GROUPS, lg, NEG)
    cmax = jnp.max(cm, axis=-1, keepdims=True)
    gi = jnp.min(jnp.where(cm == cmax, lane, ROUTER_LANES), axis=-1, keepdims=True)
    oh = (lane == gi).astype(F32)
    ohb = oh.astype(BF16)
    before = _dot(tri_ref[...], ohb)
    sel = (before * oh).astype(BF16)
    lane8 = lax.broadcasted_iota(I32, (SUBLANES, ROUTER_LANES), 1)
    rank_row = lax.dot_general(jnp.ones((SUBLANES, ROUTER_LANES), BF16), sel, NT_DIMS,
                               preferred_element_type=F32)
    g_row = lax.dot_general(lane8.astype(F32).astype(BF16), ohb, NT_DIMS, preferred_element_type=F32)
    code = g_row[0:1, :] * float(tc) + rank_row[0:1, :]
    code_ref[0] = code.astype(I32)
    cnt_ref[0] = jnp.sum(oh, axis=0, keepdims=True).astype(I32)


def _ssm_tables(a_re, a_im, log_step, b_re, b_im, c_re, c_im, seg):
    g, p = a_re.shape
    h = b_re.shape[-1]
    n_slab = g // SLAB_GROUPS
    lr, li = a_re.astype(F32), a_im.astype(F32)
    step = jnp.exp(log_step.astype(F32))[:, None]
    mag = jnp.exp(lr * step)
    ab_re = mag * jnp.cos(li * step)
    ab_im = mag * jnp.sin(li * step)
    den = lr * lr + li * li
    nr, ni = ab_re - 1.0, ab_im
    q_re = (nr * lr + ni * li) / den
    q_im = (ni * lr - nr * li) / den
    br, bi = b_re.astype(F32), b_im.astype(F32)
    bb_re = q_re[..., None] * br - q_im[..., None] * bi
    bb_im = q_re[..., None] * bi + q_im[..., None] * br

    eye = jnp.eye(SLAB_GROUPS, dtype=F32)

    def b_slab(bb):
        bb = bb.reshape(n_slab, SLAB_GROUPS, p, h)
        w = jnp.einsum('mgph,gk->mghkp', bb, eye)
        return w.reshape(n_slab, SLAB_GROUPS * h, SLAB_GROUPS * p)

    wb = jnp.concatenate([b_slab(bb_re), b_slab(bb_im)], axis=-1).astype(BF16)

    def c_slab(cc):
        cc = cc.astype(F32).reshape(n_slab, SLAB_GROUPS, h, p)
        w = jnp.einsum('mghp,gk->mgpkh', cc, eye)
        return w.reshape(n_slab, SLAB_GROUPS * p, SLAB_GROUPS * h)

    wc_re = c_slab(c_re).astype(BF16)
    wc_im = c_slab(-c_im).astype(BF16)

    def apow(nn):
        nn = nn.astype(F32)[:, None, None]
        m_ = jnp.exp(nn * (lr * step))
        ang = nn * (li * step)
        return (m_ * jnp.cos(ang)).reshape(-1, g * p), (m_ * jnp.sin(ang)).reshape(-1, g * p)

    a1r, a1i = apow(jnp.ones((1,), F32))
    a_r = jnp.broadcast_to(a1r, (SUBLANES, g * p))
    a_i = jnp.broadcast_to(a1i, (SUBLANES, g * p))
    pr, pi = apow(jnp.arange(1, seg + 1))
    ap_r = jnp.repeat(pr, SUBLANES, axis=0)
    ap_i = jnp.repeat(pi, SUBLANES, axis=0)
    ks = jnp.array([1, 2, 4])
    mr, mi = apow(ks * seg)
    rows = jnp.arange(SUBLANES)[None, :, None]
    mask = (rows >= ks[:, None, None]).astype(F32)
    mk_r = mr[:, None, :] * mask
    mk_i = mi[:, None, :] * mask
    sr, si = apow(jnp.full((1,), seg, F32))
    as_r = jnp.broadcast_to(sr, (SUBLANES, g * p))
    as_i = jnp.broadcast_to(si, (SUBLANES, g * p))
    return wb, a_r, a_i, ap_r, ap_i, mk_r, mk_i, as_r, as_i, wc_re, wc_im


def _mixer(x, norm_mix, w_in, pool_w, pool_scale, a_re, a_im, log_step, b_re, b_im, c_re, c_im,
           d_skip, glu_w, glu_b, w_out, norm_ffn, w_coarse, b_coarse):
    b, l, d = x.shape
    d_pool = pool_scale.shape[-1]
    d_ssm = d_skip.shape[-1]
    tc = TIME_TILE
    assert tc == 1 << TIME_TILE_LOG2 and l % tc == 0
    assert d_pool == len(POOL_WINDOWS) * POOL_GROUP and d_ssm % LANES == 0
    seg = tc // SUBLANES
    n_state = a_re.shape[0] * a_re.shape[1]
    tables = _ssm_tables(a_re, a_im, log_step, b_re, b_im, c_re, c_im, seg)
    wb, a_r, a_i, ap_r, ap_i, mk_r, mk_i, as_r, as_i, wc_re, wc_im = tables

    i = jnp.arange(tc)
    src = (i % SUBLANES) * seg + i // SUBLANES
    perm = (src[:, None] == i[None, :]).astype(BF16)
    permt = perm.T
    tri = (i[None, :] < i[:, None]).astype(BF16)

    pad = ROUTER_LANES - N_EXPERT_GROUPS
    wrc = jnp.concatenate([w_coarse, jnp.zeros((d, pad), F32)], axis=-1).astype(BF16)
    brc = jnp.concatenate([b_coarse, jnp.zeros((pad,), F32)]).reshape(1, -1).astype(F32)

    row2 = lambda a: a.reshape(1, -1).astype(F32)
    operands = [
        x, row2(norm_mix), w_in.astype(BF16), pool_w.astype(BF16), row2(pool_scale), perm, permt, wb,
        a_r, a_i, ap_r, ap_i, mk_r, mk_i, as_r, as_i, wc_re, wc_im,
        row2(d_skip), glu_w.astype(BF16), row2(glu_b), w_out.astype(BF16),
        row2(norm_ffn), wrc, brc, tri,
    ]

    def whole(a):
        nd = a.ndim
        return pl.BlockSpec(a.shape, lambda bi, ni, nd=nd: (0,) * nd)

    n_t = l // tc
    in_specs = [pl.BlockSpec((1, tc, d), lambda bi, ni: (bi, ni, 0))] + [whole(a) for a in operands[1:]]
    kern = functools.partial(_mixer_kernel, tc=tc, d_pool=d_pool, d_ssm=d_ssm)
    return pl.pallas_call(
        kern,
        grid=(b, n_t),
        in_specs=in_specs,
        out_specs=[
            pl.BlockSpec((1, tc, d), lambda bi, ni: (bi, ni, 0)),
            pl.BlockSpec((1, 1, tc), lambda bi, ni: (bi * n_t + ni, 0, 0)),
            pl.BlockSpec((1, 1, ROUTER_LANES), lambda bi, ni: (bi * n_t + ni, 0, 0)),
        ],
        out_shape=[
            jax.ShapeDtypeStruct((b, l, d), F32),
            jax.ShapeDtypeStruct((b * n_t, 1, tc), I32),
            jax.ShapeDtypeStruct((b * n_t, 1, ROUTER_LANES), I32),
        ],
        scratch_shapes=[
            pltpu.VMEM((POOL_TAIL + tc, d_pool), F32),
            pltpu.VMEM((tc, n_state), F32),
            pltpu.VMEM((tc, n_state), F32),
            pltpu.VMEM((SUBLANES, n_state), F32),
            pltpu.VMEM((SUBLANES, n_state), F32),
        ],
        compiler_params=pltpu.CompilerParams(
            dimension_semantics=("arbitrary", "arbitrary"), vmem_limit_bytes=VMEM_LIMIT),
        name="mixer",
    )(*operands)


def _sortmeta_kernel(code_ref, cnt_ref, tok_ref, tg_ref, nv_ref, base_ref, *, n_mt, n_tok, n_tiles):
    ng = N_EXPERT_GROUPS
    zero = jnp.int32(0)

    def tot_body(m, c):
        return tuple(c[g] + cnt_ref[m * ng + g] for g in range(ng))

    tot = lax.fori_loop(0, n_mt, tot_body, (zero,) * ng)
    ntile = [(tot[g] + (MOE_ROWS - 1)) >> MOE_ROWS_LOG2 for g in range(ng)]
    toff = [zero]
    for g in range(ng):
        toff.append(toff[g] + ntile[g])
    goff = [toff[g] << MOE_ROWS_LOG2 for g in range(ng)]

    def base_body(m, run):
        out = []
        for g in range(ng):
            base_ref[m * ng + g] = run[g]
            out.append(run[g] + cnt_ref[m * ng + g])
        return tuple(out)

    lax.fori_loop(0, n_mt, base_body, tuple(goff))

    def tile_body(k, c):
        g = zero
        for j in range(1, ng):
            g = g + (k >= toff[j]).astype(I32)
        tot_g, toff_g = tot[0], toff[0]
        for j in range(1, ng):
            tot_g = jnp.where(g == j, tot[j], tot_g)
            toff_g = jnp.where(g == j, toff[j], toff_g)
        used = k < toff[ng]
        left = tot_g - ((k - toff_g) << MOE_ROWS_LOG2)
        tg_ref[k] = jnp.where(used, g, ng - 1)
        nv_ref[k] = jnp.where(used, jnp.minimum(left, MOE_ROWS), 0)
        return c

    lax.fori_loop(0, n_tiles, tile_body, 0)

    def pad_body(p, c):
        tok_ref[p] = 0
        return c

    for g in range(ng):
        lax.fori_loop(goff[g] + tot[g], goff[g] + (ntile[g] << MOE_ROWS_LOG2), pad_body, 0)
    lax.fori_loop(toff[ng] << MOE_ROWS_LOG2, n_tiles << MOE_ROWS_LOG2, pad_body, 0)

    def scatter_body(t, c):
        cd = code_ref[t]
        g = cd >> TIME_TILE_LOG2
        rank = cd & (TIME_TILE - 1)
        pos = base_ref[(t >> TIME_TILE_LOG2) * ng + g] + rank
        tok_ref[pos] = t
        return c

    lax.fori_loop(0, n_tok, scatter_body, 0, unroll=8)


def _sortmeta(code, cnt, n_tiles):
    n_tok = code.shape[0]
    n_mt = cnt.shape[0] // N_EXPERT_GROUPS
    smem = pl.BlockSpec(memory_space=pltpu.SMEM)
    kern = functools.partial(_sortmeta_kernel, n_mt=n_mt, n_tok=n_tok, n_tiles=n_tiles)
    return pl.pallas_call(
        kern,
        in_specs=[smem, smem],
        out_specs=[smem, smem, smem],
        out_shape=[
            jax.ShapeDtypeStruct((n_tiles * MOE_ROWS,), I32),
            jax.ShapeDtypeStruct((n_tiles,), I32),
            jax.ShapeDtypeStruct((n_tiles,), I32),
        ],
        scratch_shapes=[pltpu.SMEM((n_mt * N_EXPERT_GROUPS,), I32)],
        name="sortmeta",
    )(code, cnt)


def _moe_kernel(tg_ref, nv_ref, tok_hbm, x_hbm, nf_ref, wr_ref, br_ref, wg_ref, wu_ref, wd_ref, nfin_ref,
                o_hbm, idx, xbuf, obuf, isem, gsem, ssem):
    i = pl.program_id(0)
    nt = pl.num_programs(0)
    slot = lax.rem(i, 2)
    nslot = 1 - slot
    rows = MOE_ROWS

    def idx_copy(tile, s3):
        return pltpu.make_async_copy(tok_hbm.at[tile], idx.at[pl.ds(s3 * rows, rows)], isem.at[s3])

    def gather_start(s3, bslot):
        base = s3 * rows

        def body(r, c):
            t = idx[base + r]
            pltpu.make_async_copy(x_hbm.at[pl.ds(t, 1)], xbuf.at[bslot, pl.ds(r, 1)], gsem.at[bslot]).start()
            return c

        lax.fori_loop(0, rows, body, 0, unroll=8)

    def gather_wait(bslot):
        pltpu.make_async_copy(x_hbm.at[pl.ds(0, rows)], xbuf.at[bslot], gsem.at[bslot]).wait()

    def scatter_start(s3, bslot, n):
        base = s3 * rows

        def body(r, c):
            t = idx[base + r]
            pltpu.make_async_copy(obuf.at[bslot, pl.ds(r, 1)], o_hbm.at[pl.ds(t, 1)], ssem.at[bslot]).start()
            return c

        lax.fori_loop(0, n, body, 0)

    def scatter_wait(bslot, n):
        @pl.when(n == rows)
        def _():
            pltpu.make_async_copy(obuf.at[bslot], o_hbm.at[pl.ds(0, rows)], ssem.at[bslot]).wait()

        @pl.when(n < rows)
        def _():
            def body(r, c):
                pltpu.make_async_copy(obuf.at[bslot, pl.ds(0, 1)], o_hbm.at[pl.ds(0, 1)], ssem.at[bslot]).wait()
                return c

            lax.fori_loop(0, n, body, 0)

    nv = nv_ref[i]
    nv_next = nv_ref[jnp.minimum(i + 1, nt - 1)]
    nv_prev = nv_ref[jnp.maximum(i - 1, 0)]
    nv_prev2 = nv_ref[jnp.maximum(i - 2, 0)]

    @pl.when(i == 0)
    def _():
        idx_copy(0, 0).start()
        idx_copy(0, 0).wait()

        @pl.when(nt > 1)
        def _():
            idx_copy(1, 1).start()
            idx_copy(1, 1).wait()

        @pl.when(nv > 0)
        def _():
            gather_start(0, 0)

    @pl.when(i + 2 < nt)
    def _():
        idx_copy(i + 2, lax.rem(i + 2, 3)).start()

    @pl.when((i >= 1) & (i + 1 < nt))
    def _():
        idx_copy(i + 1, lax.rem(i + 1, 3)).wait()

    @pl.when((i + 1 < nt) & (nv_next > 0))
    def _():
        gather_start(lax.rem(i + 1, 3), nslot)

    @pl.when((i >= 2) & (nv_prev2 > 0))
    def _():
        scatter_wait(slot, nv_prev2)

    @pl.when(nv > 0)
    def _():
        gather_wait(slot)
        g = tg_ref[i]
        x = xbuf[slot]
        hb = _rms(x, nf_ref[...]).astype(BF16)
        logits = _dot(hb, wr_ref[...]) + br_ref[...]
        lane = lax.broadcasted_iota(I32, (rows, ROUTER_LANES), 1)
        is_c = lane < N_EXPERT_GROUPS
        cmax = jnp.max(jnp.where(is_c, logits, NEG), axis=-1, keepdims=True)
        psum = jnp.sum(jnp.where(is_c, jnp.exp(logits - cmax), 0.0), axis=-1, keepdims=True)
        cg = jnp.sum(jnp.where(lane == g, logits, 0.0), axis=-1, keepdims=True)
        p_g = jnp.exp(cg - cmax) / psum
        lo = N_EXPERT_GROUPS + g * EXPERTS_PER_GROUP
        insel = (lane >= lo) & (lane < lo + EXPERTS_PER_GROUP)
        fs = jnp.where(insel, logits, NEG)
        v0 = jnp.max(fs, axis=-1, keepdims=True)
        i0 = jnp.min(jnp.where(insel & (fs == v0), lane, ROUTER_LANES), axis=-1, keepdims=True)
        rest = insel & (lane != i0)
        fs2 = jnp.where(rest, logits, NEG)
        v1 = jnp.max(fs2, axis=-1, keepdims=True)
        i1 = jnp.min(jnp.where(rest & (fs2 == v1), lane, ROUTER_LANES), axis=-1, keepdims=True)
        ex = jnp.exp(v1 - v0)
        den = 1.0 + ex
        w0 = p_g * (1.0 / den)
        w1 = p_g * (ex / den)
        y = jnp.zeros_like(x)
        for e in range(EXPERTS_PER_GROUP):
            ge = jnp.where(i0 == lo + e, w0, 0.0) + jnp.where(i1 == lo + e, w1, 0.0)
            a = jax.nn.silu(_dot(hb, wg_ref[e])) * _dot(hb, wu_ref[e])
            y = y + _dot((a * ge).astype(BF16), wd_ref[e])
        obuf[slot] = _rms(x + y, nfin_ref[...])
        scatter_start(lax.rem(i, 3), slot, nv)

    @pl.when(i == nt - 1)
    def _():
        @pl.when((nt > 1) & (nv_prev > 0))
        def _():
            scatter_wait(nslot, nv_prev)

        @pl.when(nv > 0)
        def _():
            scatter_wait(slot, nv)


def _moe(x1, tok, tile_group, tile_nv, norm_ffn, w_coarse, b_coarse, w_fine, b_fine, w_gate, w_up, w_down,
         norm_final):
    t, d = x1.shape
    n_e, _, f = w_gate.shape
    n_tiles = tile_group.shape[0]
    assert n_e == N_EXPERTS
    pad = ROUTER_LANES - N_EXPERT_GROUPS - N_EXPERTS
    wr = jnp.concatenate([w_coarse, w_fine, jnp.zeros((d, pad), F32)], axis=-1).astype(BF16)
    br = jnp.concatenate([b_coarse, b_fine, jnp.zeros((pad,), F32)]).reshape(1, -1).astype(F32)
    row2 = lambda a: a.reshape(1, -1).astype(F32)
    const = lambda shape: pl.BlockSpec(shape, lambda i, tg, nv: (0,) * len(shape))
    anyspec = pl.BlockSpec(memory_space=pl.ANY)
    epg = EXPERTS_PER_GROUP
    grid_spec = pltpu.PrefetchScalarGridSpec(
        num_scalar_prefetch=2,
        grid=(n_tiles,),
        in_specs=[
            anyspec, anyspec,
            const((1, d)), const((d, ROUTER_LANES)), const((1, ROUTER_LANES)),
            pl.BlockSpec((epg, d, f), lambda i, tg, nv: (tg[i], 0, 0)),
            pl.BlockSpec((epg, d, f), lambda i, tg, nv: (tg[i], 0, 0)),
            pl.BlockSpec((epg, f, d), lambda i, tg, nv: (tg[i], 0, 0)),
            const((1, d)),
        ],
        out_specs=anyspec,
        scratch_shapes=[
            pltpu.SMEM((3 * MOE_ROWS,), I32),
            pltpu.VMEM((2, MOE_ROWS, d), F32),
            pltpu.VMEM((2, MOE_ROWS, d), F32),
            pltpu.SemaphoreType.DMA((3,)),
            pltpu.SemaphoreType.DMA((2,)),
            pltpu.SemaphoreType.DMA((2,)),
        ],
    )
    return pl.pallas_call(
        _moe_kernel,
        grid_spec=grid_spec,
        out_shape=jax.ShapeDtypeStruct((t, d), F32),
        compiler_params=pltpu.CompilerParams(
            dimension_semantics=("arbitrary",), vmem_limit_bytes=VMEM_LIMIT),
        name="moe",
    )(tile_group, tile_nv, tok.reshape(n_tiles, MOE_ROWS), x1, row2(norm_ffn), wr, br,
      w_gate.astype(BF16), w_up.astype(BF16), w_down.astype(BF16), row2(norm_final))


def kernel(x, norm_mix, w_in, pool_w, pool_scale, ssm_a_re, ssm_a_im, ssm_log_step, ssm_b_re, ssm_b_im, ssm_c_re, ssm_c_im, ssm_d, glu_w, glu_b, w_out, norm_ffn, router_coarse_w, router_coarse_b, router_fine_w, router_fine_b, exp_w_gate, exp_w_up, exp_w_down, norm_final):
    assert norm_mix.shape[0] == 1, "the moe call fuses the final norm: single-layer blocks only"
    b, l, d = x.shape
    t = b * l
    assert t % MOE_ROWS == 0
    x1, code, cnt = _mixer(x, norm_mix[0], w_in[0], pool_w[0], pool_scale[0], ssm_a_re[0], ssm_a_im[0],
                           ssm_log_step[0], ssm_b_re[0], ssm_b_im[0], ssm_c_re[0], ssm_c_im[0], ssm_d[0],
                           glu_w[0], glu_b[0], w_out[0], norm_ffn[0], router_coarse_w[0], router_coarse_b[0])
    n_tiles = t // MOE_ROWS + N_EXPERT_GROUPS
    tok, tile_group, tile_nv = _sortmeta(code.reshape(t), cnt[:, 0, :N_EXPERT_GROUPS].reshape(-1), n_tiles)
    y = _moe(x1.reshape(t, d), tok, tile_group, tile_nv, norm_ffn[0], router_coarse_w[0], router_coarse_b[0],
             router_fine_w[0], router_fine_b[0], exp_w_gate[0], exp_w_up[0], exp_w_down[0], norm_final)
    return y.reshape(b, l, d)
```

```python
import functools

import jax
import jax.numpy as jnp
from jax import lax
from jax.experimental import pallas as pl
from jax.experimental.pallas import tpu as pltpu

EPS = 1e-6
POOL_WINDOWS = (2, 4, 8, 16)
POOL_GROUP = 128
POOL_TAIL = 16
SSM_GROUP = 16
SSM_STATE = 64
N_EXPERT_GROUPS = 4
EXPERTS_PER_GROUP = 4
N_EXPERTS = 16

SUBLANES = 8
LANES = 128
SLAB_GROUPS = 8
STATE_SLAB = SLAB_GROUPS * SSM_STATE
TIME_TILE = 256
TIME_TILE_LOG2 = 8
MOE_ROWS = 256
MOE_ROWS_LOG2 = 8
ROUTER_LANES = 128
VMEM_LIMIT = 56 * 1024 * 1024
NEG = -3.0e38

F32 = jnp.float32
BF16 = jnp.bfloat16
I32 = jnp.int32
NT_DIMS = (((1,), (1,)), ((), ()))


def _dot(a, b):
    return jnp.dot(a, b, preferred_element_type=F32)


def _rms(x, g):
    return x * lax.rsqrt(jnp.mean(x * x, axis=-1, keepdims=True) + EPS) * g


def _mixer_kernel(x_ref, nm_ref, win_ref, pw_ref, ps_ref, perm_ref, permt_ref, wb_ref,
                  are_ref, aim_ref, apre_ref, apim_ref, mkre_ref, mkim_ref, asre_ref, asim_ref,
                  wcre_ref, wcim_ref, d_ref, gw_ref, gb_ref, wout_ref, nf_ref, wrc_ref, brc_ref, tri_ref,
                  xs_hbm, code_ref, tab_ref, tgrp_ref, tnv_ref,
                  zpad, bur, bui, st_r, st_i, x1t, zbuf, meta_v, meta_s, open_s, fill_s, free_s, lo_s, hi_s,
                  ssem, msem, zsem, *, tc, d_pool, d_ssm, n_tiles):
    bq = pl.program_id(0)
    n = pl.program_id(1)
    n_t = pl.num_programs(1)
    step = bq * n_t + n
    n_steps = pl.num_programs(0) * n_t
    slot = lax.rem(step, 2)
    seg = tc // SUBLANES
    n_slab = d_ssm // LANES
    ng = N_EXPERT_GROUPS

    @pl.when(n == 0)
    def _():
        zpad[0:POOL_TAIL, :] = jnp.zeros((POOL_TAIL, d_pool), F32)
        st_r[...] = jnp.zeros_like(st_r)
        st_i[...] = jnp.zeros_like(st_i)

    @pl.when(step == 0)
    def _():
        for g in range(ng):
            open_s[g] = g
            fill_s[g] = 0
            tgrp_ref[g] = g
        free_s[0] = ng

    x = x_ref[0]
    hn = _rms(x, nm_ref[...]).astype(BF16)
    z = _dot(hn, win_ref[...])

    zp = z[:, :d_pool]
    zpad[POOL_TAIL:POOL_TAIL + tc, :] = zp
    t = n * tc + lax.broadcasted_iota(I32, (tc, 1), 0)
    pooled = []
    for gi, w in enumerate(POOL_WINDOWS):
        cols = slice(gi * POOL_GROUP, (gi + 1) * POOL_GROUP)
        acc = zp[:, cols]
        for k in range(1, w):
            acc = acc + zpad[POOL_TAIL - k:POOL_TAIL - k + tc, cols]
        cnt = jnp.minimum(t + 1, w).astype(F32)
        p = acc / cnt - zp[:, cols]
        pooled.append(_dot(p.astype(BF16), pw_ref[gi]))
    y_pool = jnp.concatenate(pooled, axis=-1) * ps_ref[...]
    zpad[0:POOL_TAIL, :] = zpad[tc:tc + POOL_TAIL, :]

    u = z[:, d_pool:]
    up = _dot(perm_ref[...], u.astype(BF16))
    upb = up.astype(BF16)
    for m in range(n_slab):
        r = _dot(upb[:, m * LANES:(m + 1) * LANES], wb_ref[m])
        bur[:, m * STATE_SLAB:(m + 1) * STATE_SLAB] = r[:, :STATE_SLAB]
        bui[:, m * STATE_SLAB:(m + 1) * STATE_SLAB] = r[:, STATE_SLAB:]

    row = lax.broadcasted_iota(I32, (SUBLANES, STATE_SLAB), 0)
    for c in range(n_slab):
        cs = slice(c * STATE_SLAB, (c + 1) * STATE_SLAB)
        ar = are_ref[:, cs]
        ai = aim_ref[:, cs]

        def scan_body(s, carry, cs=cs, ar=ar, ai=ai):
            xr, xi = carry
            r0 = pl.multiple_of(s * SUBLANES, SUBLANES)
            nxr = ar * xr - ai * xi + bur[pl.ds(r0, SUBLANES), cs]
            nxi = ar * xi + ai * xr + bui[pl.ds(r0, SUBLANES), cs]
            bur[pl.ds(r0, SUBLANES), cs] = nxr
            bui[pl.ds(r0, SUBLANES), cs] = nxi
            return nxr, nxi

        zero = jnp.zeros((SUBLANES, STATE_SLAB), F32)
        er, ei = lax.fori_loop(0, seg, scan_body, (zero, zero), unroll=4)

        fr = jnp.where(row == 0, st_r[:, cs], pltpu.roll(er, 1, 0))
        fi = jnp.where(row == 0, st_i[:, cs], pltpu.roll(ei, 1, 0))
        for ki, k in enumerate((1, 2, 4)):
            mr = mkre_ref[ki, :, cs]
            mi = mkim_ref[ki, :, cs]
            rr = pltpu.roll(fr, k, 0)
            ri = pltpu.roll(fi, k, 0)
            fr, fi = fr + mr * rr - mi * ri, fi + mr * ri + mi * rr
        asr = asre_ref[:, cs]
        asi = asim_ref[:, cs]
        nsr = asr * fr - asi * fi + er
        nsi = asr * fi + asi * fr + ei
        st_r[:, cs] = pltpu.roll(nsr, 1, 0)
        st_i[:, cs] = pltpu.roll(nsi, 1, 0)

        def fix_body(s, carry, cs=cs, fr=fr, fi=fi):
            r0 = pl.multiple_of(s * SUBLANES, SUBLANES)
            pr = apre_ref[pl.ds(r0, SUBLANES), cs]
            pi = apim_ref[pl.ds(r0, SUBLANES), cs]
            bur[pl.ds(r0, SUBLANES), cs] = bur[pl.ds(r0, SUBLANES), cs] + pr * fr - pi * fi
            bui[pl.ds(r0, SUBLANES), cs] = bui[pl.ds(r0, SUBLANES), cs] + pr * fi + pi * fr
            return carry

        lax.fori_loop(0, seg, fix_body, 0, unroll=4)

    ys = []
    for m in range(n_slab):
        cs = slice(m * STATE_SLAB, (m + 1) * STATE_SLAB)
        ys.append(_dot(bur[:, cs].astype(BF16), wcre_ref[m]) + _dot(bui[:, cs].astype(BF16), wcim_ref[m]))
    y = jnp.concatenate(ys, axis=-1) + d_ref[...] * up
    y = jax.nn.gelu(y)
    y = y * jax.nn.sigmoid(_dot(y.astype(BF16), gw_ref[...]) + gb_ref[...])
    y_ssm = _dot(permt_ref[...], y.astype(BF16))

    mix = _dot(y_pool.astype(BF16), wout_ref[0:d_pool, :]) + _dot(y_ssm.astype(BF16), wout_ref[d_pool:, :])
    x1 = x + mix

    hb = _rms(x1, nf_ref[...]).astype(BF16)
    lg = _dot(hb, wrc_ref[...]) + brc_ref[...]
    lane = lax.broadcasted_iota(I32, (tc, ROUTER_LANES), 1)
    cm = jnp.where(lane < N_EXPERT_GROUPS, lg, NEG)
    cmax = jnp.max(cm, axis=-1, keepdims=True)
    gi = jnp.min(jnp.where(cm == cmax, lane, ROUTER_LANES), axis=-1, keepdims=True)
    oh = (lane == gi).astype(F32)
    ohb = oh.astype(BF16)
    before = _dot(tri_ref[...], ohb)
    sel = (before * oh).astype(BF16)
    lane8 = lax.broadcasted_iota(I32, (SUBLANES, ROUTER_LANES), 1)
    rank_row = lax.dot_general(jnp.ones((SUBLANES, ROUTER_LANES), BF16), sel, NT_DIMS,
                               preferred_element_type=F32)
    g_row = lax.dot_general(lane8.astype(F32).astype(BF16), ohb, NT_DIMS, preferred_element_type=F32)
    code = (g_row[0:1, :] * float(tc) + rank_row[0:1, :]).astype(I32)
    cnt = jnp.sum(oh, axis=0, keepdims=True).astype(I32)
    code_ref[0] = code

    full = pltpu.make_async_copy(x1t.at[slot], xs_hbm.at[pl.ds(0, tc * SUBLANES)], ssem.at[slot])

    @pl.when(step >= 2)
    def _():
        full.wait()

    for c in range(SUBLANES):
        x1t[slot, pl.ds(c, tc, stride=SUBLANES), :] = x1[:, c * LANES:(c + 1) * LANES]
    meta_v[0:1, 0:tc] = code
    meta_v[0:1, tc:tc + ROUTER_LANES] = cnt
    meta_cp = pltpu.make_async_copy(meta_v.at[0], meta_s, msem)
    meta_cp.start()
    meta_cp.wait()

    for g in range(ng):
        c_g = meta_s[tc + g]
        fill = fill_s[g]
        room = MOE_ROWS - fill
        lo = (open_s[g] << MOE_ROWS_LOG2) + fill
        spill = c_g > room
        nxt = free_s[0]
        hi = (nxt << MOE_ROWS_LOG2) - room
        e = (step * ng + g) * 3
        tab_ref[e] = lo
        tab_ref[e + 1] = hi
        tab_ref[e + 2] = room
        lo_s[g] = lo
        hi_s[g] = hi
        meta_s[tc + g] = room

        @pl.when(spill)
        def _(g=g, nxt=nxt, c_g=c_g, room=room):
            tgrp_ref[nxt] = g
            open_s[g] = nxt
            fill_s[g] = c_g - room
            free_s[0] = nxt + 1

        @pl.when(jnp.logical_not(spill))
        def _(g=g, c_g=c_g, fill=fill):
            fill_s[g] = fill + c_g

    def send(r, carry):
        cd = meta_s[r]
        g = cd >> TIME_TILE_LOG2
        rk = cd & (tc - 1)
        dst = jnp.where(rk < meta_s[tc + g], lo_s[g], hi_s[g]) + rk
        pltpu.make_async_copy(x1t.at[slot, pl.ds(pl.multiple_of(r * SUBLANES, SUBLANES), SUBLANES)],
                              xs_hbm.at[pl.ds(pl.multiple_of(dst * SUBLANES, SUBLANES), SUBLANES)],
                              ssem.at[slot]).start()
        return carry

    lax.fori_loop(0, tc, send, 0, unroll=8)

    @pl.when(step == n_steps - 1)
    def _():
        zbuf[...] = jnp.zeros_like(zbuf)

        def zcopy(p):
            return pltpu.make_async_copy(
                zbuf, xs_hbm.at[pl.ds(pl.multiple_of(p * SUBLANES, SUBLANES), SUBLANES)], zsem)

        def zstart(p, carry):
            zcopy(p).start()
            return carry

        def zwait(p, carry):
            zcopy(p).wait()
            return carry

        used = free_s[0]

        def nv_body(k, carry):
            tnv_ref[k] = jnp.where(k < used, MOE_ROWS, 0)
            return carry

        lax.fori_loop(0, n_tiles, nv_body, 0)

        def tail_body(k, carry):
            tgrp_ref[k] = 0
            return carry

        lax.fori_loop(used, n_tiles, tail_body, 0)
        for g in range(ng):
            tile = open_s[g]
            fill = fill_s[g]
            tnv_ref[tile] = fill
            lax.fori_loop((tile << MOE_ROWS_LOG2) + fill, (tile + 1) << MOE_ROWS_LOG2, zstart, 0)
            lax.fori_loop((tile << MOE_ROWS_LOG2) + fill, (tile + 1) << MOE_ROWS_LOG2, zwait, 0)
        lax.fori_loop(used << MOE_ROWS_LOG2, n_tiles << MOE_ROWS_LOG2, zstart, 0)
        lax.fori_loop(used << MOE_ROWS_LOG2, n_tiles << MOE_ROWS_LOG2, zwait, 0)

        full.wait()

        @pl.when(step >= 1)
        def _():
            pltpu.make_async_copy(x1t.at[1 - slot], xs_hbm.at[pl.ds(0, tc * SUBLANES)], ssem.at[1 - slot]).wait()


def _ssm_tables(a_re, a_im, log_step, b_re, b_im, c_re, c_im, seg):
    g, p = a_re.shape
    h = b_re.shape[-1]
    n_slab = g // SLAB_GROUPS
    lr, li = a_re.astype(F32), a_im.astype(F32)
    step = jnp.exp(log_step.astype(F32))[:, None]
    mag = jnp.exp(lr * step)
    ab_re = mag * jnp.cos(li * step)
    ab_im = mag * jnp.sin(li * step)
    den = lr * lr + li * li
    nr, ni = ab_re - 1.0, ab_im
    q_re = (nr * lr + ni * li) / den
    q_im = (ni * lr - nr * li) / den
    br, bi = b_re.astype(F32), b_im.astype(F32)
    bb_re = q_re[..., None] * br - q_im[..., None] * bi
    bb_im = q_re[..., None] * bi + q_im[..., None] * br

    eye = jnp.eye(SLAB_GROUPS, dtype=F32)

    def b_slab(bb):
        bb = bb.reshape(n_slab, SLAB_GROUPS, p, h)
        w = jnp.einsum('mgph,gk->mghkp', bb, eye)
        return w.reshape(n_slab, SLAB_GROUPS * h, SLAB_GROUPS * p)

    wb = jnp.concatenate([b_slab(bb_re), b_slab(bb_im)], axis=-1).astype(BF16)

    def c_slab(cc):
        cc = cc.astype(F32).reshape(n_slab, SLAB_GROUPS, h, p)
        w = jnp.einsum('mghp,gk->mgpkh', cc, eye)
        return w.reshape(n_slab, SLAB_GROUPS * p, SLAB_GROUPS * h)

    wc_re = c_slab(c_re).astype(BF16)
    wc_im = c_slab(-c_im).astype(BF16)

    def apow(nn):
        nn = nn.astype(F32)[:, None, None]
        m_ = jnp.exp(nn * (lr * step))
        ang = nn * (li * step)
        return (m_ * jnp.cos(ang)).reshape(-1, g * p), (m_ * jnp.sin(ang)).reshape(-1, g * p)

    a1r, a1i = apow(jnp.ones((1,), F32))
    a_r = jnp.broadcast_to(a1r, (SUBLANES, g * p))
    a_i = jnp.broadcast_to(a1i, (SUBLANES, g * p))
    pr, pi = apow(jnp.arange(1, seg + 1))
    ap_r = jnp.repeat(pr, SUBLANES, axis=0)
    ap_i = jnp.repeat(pi, SUBLANES, axis=0)
    ks = jnp.array([1, 2, 4])
    mr, mi = apow(ks * seg)
    rows = jnp.arange(SUBLANES)[None, :, None]
    mask = (rows >= ks[:, None, None]).astype(F32)
    mk_r = mr[:, None, :] * mask
    mk_i = mi[:, None, :] * mask
    sr, si = apow(jnp.full((1,), seg, F32))
    as_r = jnp.broadcast_to(sr, (SUBLANES, g * p))
    as_i = jnp.broadcast_to(si, (SUBLANES, g * p))
    return wb, a_r, a_i, ap_r, ap_i, mk_r, mk_i, as_r, as_i, wc_re, wc_im


def _mixer(x, norm_mix, w_in, pool_w, pool_scale, a_re, a_im, log_step, b_re, b_im, c_re, c_im,
           d_skip, glu_w, glu_b, w_out, norm_ffn, w_coarse, b_coarse):
    b, l, d = x.shape
    d_pool = pool_scale.shape[-1]
    d_ssm = d_skip.shape[-1]
    tc = TIME_TILE
    assert tc == 1 << TIME_TILE_LOG2 and l % tc == 0
    assert d_pool == len(POOL_WINDOWS) * POOL_GROUP and d_ssm % LANES == 0
    seg = tc // SUBLANES
    n_state = a_re.shape[0] * a_re.shape[1]
    tables = _ssm_tables(a_re, a_im, log_step, b_re, b_im, c_re, c_im, seg)
    wb, a_r, a_i, ap_r, ap_i, mk_r, mk_i, as_r, as_i, wc_re, wc_im = tables

    i = jnp.arange(tc)
    src = (i % SUBLANES) * seg + i // SUBLANES
    perm = (src[:, None] == i[None, :]).astype(BF16)
    permt = perm.T
    tri = (i[None, :] < i[:, None]).astype(BF16)

    pad = ROUTER_LANES - N_EXPERT_GROUPS
    wrc = jnp.concatenate([w_coarse, jnp.zeros((d, pad), F32)], axis=-1).astype(BF16)
    brc = jnp.concatenate([b_coarse, jnp.zeros((pad,), F32)]).reshape(1, -1).astype(F32)

    row2 = lambda a: a.reshape(1, -1).astype(F32)
    operands = [
        x, row2(norm_mix), w_in.astype(BF16), pool_w.astype(BF16), row2(pool_scale), perm, permt, wb,
        a_r, a_i, ap_r, ap_i, mk_r, mk_i, as_r, as_i, wc_re, wc_im,
        row2(d_skip), glu_w.astype(BF16), row2(glu_b), w_out.astype(BF16),
        row2(norm_ffn), wrc, brc, tri,
    ]

    def whole(a):
        nd = a.ndim
        return pl.BlockSpec(a.shape, lambda bi, ni, nd=nd: (0,) * nd)

    n_t = l // tc
    in_specs = [pl.BlockSpec((1, tc, d), lambda bi, ni: (bi, ni, 0))] + [whole(a) for a in operands[1:]]
    assert d == SUBLANES * LANES, "one token row must fill exactly one (8, 128) tile"
    assert tc == MOE_ROWS, "a mixer tile must never need more than one new sorted tile per group"
    n_tiles = (b * l) // MOE_ROWS + N_EXPERT_GROUPS
    smem = pl.BlockSpec(memory_space=pltpu.SMEM)
    kern = functools.partial(_mixer_kernel, tc=tc, d_pool=d_pool, d_ssm=d_ssm, n_tiles=n_tiles)
    return pl.pallas_call(
        kern,
        grid=(b, n_t),
        in_specs=in_specs,
        out_specs=[
            pl.BlockSpec(memory_space=pl.ANY),
            pl.BlockSpec((1, 1, tc), lambda bi, ni: (bi * n_t + ni, 0, 0)),
            smem, smem, smem,
        ],
        out_shape=[
            jax.ShapeDtypeStruct((n_tiles * MOE_ROWS * SUBLANES, LANES), F32),
            jax.ShapeDtypeStruct((b * n_t, 1, tc), I32),
            jax.ShapeDtypeStruct((b * n_t * N_EXPERT_GROUPS * 3,), I32),
            jax.ShapeDtypeStruct((n_tiles,), I32),
            jax.ShapeDtypeStruct((n_tiles,), I32),
        ],
        scratch_shapes=[
            pltpu.VMEM((POOL_TAIL + tc, d_pool), F32),
            pltpu.VMEM((tc, n_state), F32),
            pltpu.VMEM((tc, n_state), F32),
            pltpu.VMEM((SUBLANES, n_state), F32),
            pltpu.VMEM((SUBLANES, n_state), F32),
            pltpu.VMEM((2, tc * SUBLANES, LANES), F32),
            pltpu.VMEM((SUBLANES, LANES), F32),
            pltpu.VMEM((SUBLANES, tc + ROUTER_LANES), I32),
            pltpu.SMEM((tc + ROUTER_LANES,), I32),
            pltpu.SMEM((N_EXPERT_GROUPS,), I32),
            pltpu.SMEM((N_EXPERT_GROUPS,), I32),
            pltpu.SMEM((1,), I32),
            pltpu.SMEM((N_EXPERT_GROUPS,), I32),
            pltpu.SMEM((N_EXPERT_GROUPS,), I32),
            pltpu.SemaphoreType.DMA((2,)),
            pltpu.SemaphoreType.DMA,
            pltpu.SemaphoreType.DMA,
        ],
        compiler_params=pltpu.CompilerParams(
            dimension_semantics=("arbitrary", "arbitrary"), vmem_limit_bytes=VMEM_LIMIT),
        name="mixer",
    )(*operands)


def _plan_kernel(cnt_ref, base_ref, tblk_ref, tgrp_ref, tnv_ref, *, n_batch, n_t, seq_len, n_tiles):
    ng = N_EXPERT_GROUPS
    blocks_per_seg = seq_len >> MOE_ROWS_LOG2
    k = jnp.int32(0)
    for b in range(n_batch):
        def base_body(n, run, b=b):
            out = []
            for g in range(ng):
                e = (b * n_t + n) * ng + g
                base_ref[e] = (b * ng + g) * seq_len + run[g]
                out.append(run[g] + cnt_ref[e])
            return tuple(out)

        tot = lax.fori_loop(0, n_t, base_body, (jnp.int32(0),) * ng)
        for g in range(ng):
            ntile = (tot[g] + (MOE_ROWS - 1)) >> MOE_ROWS_LOG2

            def tile_body(j, c, k=k, g=g, tot_g=tot[g], seg=b * ng + g):
                tblk_ref[k + j] = seg * blocks_per_seg + j
                tgrp_ref[k + j] = g
                tnv_ref[k + j] = jnp.minimum(tot_g - (j << MOE_ROWS_LOG2), MOE_ROWS)
                return c

            lax.fori_loop(0, ntile, tile_body, 0)
            k = k + ntile

    def tail_body(j, c):
        last = jnp.maximum(k - 1, 0)
        tblk_ref[j] = tblk_ref[last]
        tgrp_ref[j] = tgrp_ref[last]
        tnv_ref[j] = 0
        return c

    lax.fori_loop(k, n_tiles, tail_body, 0)


def _plan(cnt, n_batch, n_t, seq_len, n_tiles):
    smem = pl.BlockSpec(memory_space=pltpu.SMEM)
    kern = functools.partial(_plan_kernel, n_batch=n_batch, n_t=n_t, seq_len=seq_len, n_tiles=n_tiles)
    return pl.pallas_call(
        kern,
        in_specs=[smem],
        out_specs=[smem, smem, smem, smem],
        out_shape=[
            jax.ShapeDtypeStruct(cnt.shape, I32),
            jax.ShapeDtypeStruct((n_tiles,), I32),
            jax.ShapeDtypeStruct((n_tiles,), I32),
            jax.ShapeDtypeStruct((n_tiles,), I32),
        ],
        name="plan",
    )(cnt)


def _moe_kernel(tg_ref, nv_ref, tok_hbm, x_hbm, nf_ref, wr_ref, br_ref, wg_ref, wu_ref, wd_ref, nfin_ref,
                o_hbm, idx, xbuf, obuf, isem, gsem, ssem):
    i = pl.program_id(0)
    nt = pl.num_programs(0)
    slot = lax.rem(i, 2)
    nslot = 1 - slot
    rows = MOE_ROWS

    def idx_copy(tile, s3):
        return pltpu.make_async_copy(tok_hbm.at[tile], idx.at[pl.ds(s3 * rows, rows)], isem.at[s3])

    def gather_start(s3, bslot):
        base = s3 * rows

        def body(r, c):
            t = idx[base + r]
            pltpu.make_async_copy(x_hbm.at[pl.ds(t, 1)], xbuf.at[bslot, pl.ds(r, 1)], gsem.at[bslot]).start()
            return c

        lax.fori_loop(0, rows, body, 0, unroll=8)

    def gather_wait(bslot):
        pltpu.make_async_copy(x_hbm.at[pl.ds(0, rows)], xbuf.at[bslot], gsem.at[bslot]).wait()

    def scatter_start(s3, bslot, n):
        base = s3 * rows

        def body(r, c):
            t = idx[base + r]
            pltpu.make_async_copy(obuf.at[bslot, pl.ds(r, 1)], o_hbm.at[pl.ds(t, 1)], ssem.at[bslot]).start()
            return c

        lax.fori_loop(0, n, body, 0)

    def scatter_wait(bslot, n):
        @pl.when(n == rows)
        def _():
            pltpu.make_async_copy(obuf.at[bslot], o_hbm.at[pl.ds(0, rows)], ssem.at[bslot]).wait()

        @pl.when(n < rows)
        def _():
            def body(r, c):
                pltpu.make_async_copy(obuf.at[bslot, pl.ds(0, 1)], o_hbm.at[pl.ds(0, 1)], ssem.at[bslot]).wait()
                return c

            lax.fori_loop(0, n, body, 0)

    nv = nv_ref[i]
    nv_next = nv_ref[jnp.minimum(i + 1, nt - 1)]
    nv_prev = nv_ref[jnp.maximum(i - 1, 0)]
    nv_prev2 = nv_ref[jnp.maximum(i - 2, 0)]

    @pl.when(i == 0)
    def _():
        idx_copy(0, 0).start()
        idx_copy(0, 0).wait()

        @pl.when(nt > 1)
        def _():
            idx_copy(1, 1).start()
            idx_copy(1, 1).wait()

        @pl.when(nv > 0)
        def _():
            gather_start(0, 0)

    @pl.when(i + 2 < nt)
    def _():
        idx_copy(i + 2, lax.rem(i + 2, 3)).start()

    @pl.when((i >= 1) & (i + 1 < nt))
    def _():
        idx_copy(i + 1, lax.rem(i + 1, 3)).wait()

    @pl.when((i + 1 < nt) & (nv_next > 0))
    def _():
        gather_start(lax.rem(i + 1, 3), nslot)

    @pl.when((i >= 2) & (nv_prev2 > 0))
    def _():
        scatter_wait(slot, nv_prev2)

    @pl.when(nv > 0)
    def _():
        gather_wait(slot)
        g = tg_ref[i]
        x = xbuf[slot]
        hb = _rms(x, nf_ref[...]).astype(BF16)
        logits = _dot(hb, wr_ref[...]) + br_ref[...]
        lane = lax.broadcasted_iota(I32, (rows, ROUTER_LANES), 1)
        is_c = lane < N_EXPERT_GROUPS
        cmax = jnp.max(jnp.where(is_c, logits, NEG), axis=-1, keepdims=True)
        psum = jnp.sum(jnp.where(is_c, jnp.exp(logits - cmax), 0.0), axis=-1, keepdims=True)
        cg = jnp.sum(jnp.where(lane == g, logits, 0.0), axis=-1, keepdims=True)
        p_g = jnp.exp(cg - cmax) / psum
        lo = N_EXPERT_GROUPS + g * EXPERTS_PER_GROUP
        insel = (lane >= lo) & (lane < lo + EXPERTS_PER_GROUP)
        fs = jnp.where(insel, logits, NEG)
        v0 = jnp.max(fs, axis=-1, keepdims=True)
        i0 = jnp.min(jnp.where(insel & (fs == v0), lane, ROUTER_LANES), axis=-1, keepdims=True)
        rest = insel & (lane != i0)
        fs2 = jnp.where(rest, logits, NEG)
        v1 = jnp.max(fs2, axis=-1, keepdims=True)
        i1 = jnp.min(jnp.where(rest & (fs2 == v1), lane, ROUTER_LANES), axis=-1, keepdims=True)
        ex = jnp.exp(v1 - v0)
        den = 1.0 + ex
        w0 = p_g * (1.0 / den)
        w1 = p_g * (ex / den)
        y = jnp.zeros_like(x)
        for e in range(EXPERTS_PER_GROUP):
            ge = jnp.where(i0 == lo + e, w0, 0.0) + jnp.where(i1 == lo + e, w1, 0.0)
            a = jax.nn.silu(_dot(hb, wg_ref[e])) * _dot(hb, wu_ref[e])
            y = y + _dot((a * ge).astype(BF16), wd_ref[e])
        obuf[slot] = _rms(x + y, nfin_ref[...])
        scatter_start(lax.rem(i, 3), slot, nv)

    @pl.when(i == nt - 1)
    def _():
        @pl.when((nt > 1) & (nv_prev > 0))
        def _():
            scatter_wait(nslot, nv_prev)

        @pl.when(nv > 0)
        def _():
            scatter_wait(slot, nv)


def _moe(x1, tok, tile_group, tile_nv, norm_ffn, w_coarse, b_coarse, w_fine, b_fine, w_gate, w_up, w_down,
         norm_final):
    t, d = x1.shape
    n_e, _, f = w_gate.shape
    n_tiles = tile_group.shape[0]
    assert n_e == N_EXPERTS
    pad = ROUTER_LANES - N_EXPERT_GROUPS - N_EXPERTS
    wr = jnp.concatenate([w_coarse, w_fine, jnp.zeros((d, pad), F32)], axis=-1).astype(BF16)
    br = jnp.concatenate([b_coarse, b_fine, jnp.zeros((pad,), F32)]).reshape(1, -1).astype(F32)
    row2 = lambda a: a.reshape(1, -1).astype(F32)
    const = lambda shape: pl.BlockSpec(shape, lambda i, tg, nv: (0,) * len(shape))
    anyspec = pl.BlockSpec(memory_space=pl.ANY)
    epg = EXPERTS_PER_GROUP
    grid_spec = pltpu.PrefetchScalarGridSpec(
        num_scalar_prefetch=2,
        grid=(n_tiles,),
        in_specs=[
            anyspec, anyspec,
            const((1, d)), const((d, ROUTER_LANES)), const((1, ROUTER_LANES)),
            pl.BlockSpec((epg, d, f), lambda i, tg, nv: (tg[i], 0, 0)),
            pl.BlockSpec((epg, d, f), lambda i, tg, nv: (tg[i], 0, 0)),
            pl.BlockSpec((epg, f, d), lambda i, tg, nv: (tg[i], 0, 0)),
            const((1, d)),
        ],
        out_specs=anyspec,
        scratch_shapes=[
            pltpu.SMEM((3 * MOE_ROWS,), I32),
            pltpu.VMEM((2, MOE_ROWS, d), F32),
            pltpu.VMEM((2, MOE_ROWS, d), F32),
            pltpu.SemaphoreType.DMA((3,)),
            pltpu.SemaphoreType.DMA((2,)),
            pltpu.SemaphoreType.DMA((2,)),
        ],
    )
    return pl.pallas_call(
        _moe_kernel,
        grid_spec=grid_spec,
        out_shape=jax.ShapeDtypeStruct((t, d), F32),
        compiler_params=pltpu.CompilerParams(
            dimension_semantics=("arbitrary",), vmem_limit_bytes=VMEM_LIMIT),
        name="moe",
    )(tile_group, tile_nv, tok.reshape(n_tiles, MOE_ROWS), x1, row2(norm_ffn), wr, br,
      w_gate.astype(BF16), w_up.astype(BF16), w_down.astype(BF16), row2(norm_final))


def _experts_kernel(tgrp_ref, tnv_ref, x_ref, nf_ref, wr_ref, br_ref, wg_hbm, wu_hbm, wd_hbm,
                    nfin_ref, o_ref, wg_ref, wu_ref, wd_ref, wsem):
    i = pl.program_id(0)
    rows = MOE_ROWS

    @pl.when(i == 0)
    def _():
        copies = [pltpu.make_async_copy(src, dst, wsem.at[j])
                  for j, (src, dst) in enumerate(((wg_hbm, wg_ref), (wu_hbm, wu_ref), (wd_hbm, wd_ref)))]
        for cp in copies:
            cp.start()
        for cp in copies:
            cp.wait()

    @pl.when(tnv_ref[i] == 0)
    def _():
        o_ref[...] = jnp.zeros_like(o_ref)

    @pl.when(tnv_ref[i] > 0)
    def _():
        g = tgrp_ref[i]
        e0 = g * EXPERTS_PER_GROUP
        x = jnp.concatenate([x_ref[pl.ds(c, rows, stride=SUBLANES), :] for c in range(SUBLANES)], axis=-1)
        hb = _rms(x, nf_ref[...]).astype(BF16)
        logits = _dot(hb, wr_ref[...]) + br_ref[...]
        lane = lax.broadcasted_iota(I32, (rows, ROUTER_LANES), 1)
        is_c = lane < N_EXPERT_GROUPS
        cmax = jnp.max(jnp.where(is_c, logits, NEG), axis=-1, keepdims=True)
        psum = jnp.sum(jnp.where(is_c, jnp.exp(logits - cmax), 0.0), axis=-1, keepdims=True)
        cg = jnp.sum(jnp.where(lane == g, logits, 0.0), axis=-1, keepdims=True)
        p_g = jnp.exp(cg - cmax) / psum
        lo = N_EXPERT_GROUPS + g * EXPERTS_PER_GROUP
        insel = (lane >= lo) & (lane < lo + EXPERTS_PER_GROUP)
        fs = jnp.where(insel, logits, NEG)
        v0 = jnp.max(fs, axis=-1, keepdims=True)
        i0 = jnp.min(jnp.where(insel & (fs == v0), lane, ROUTER_LANES), axis=-1, keepdims=True)
        rest = insel & (lane != i0)
        fs2 = jnp.where(rest, logits, NEG)
        v1 = jnp.max(fs2, axis=-1, keepdims=True)
        i1 = jnp.min(jnp.where(rest & (fs2 == v1), lane, ROUTER_LANES), axis=-1, keepdims=True)
        ex = jnp.exp(v1 - v0)
        den = 1.0 + ex
        w0 = p_g * (1.0 / den)
        w1 = p_g * (ex / den)
        y = jnp.zeros_like(x)
        for e in range(EXPERTS_PER_GROUP):
            ge = jnp.where(i0 == lo + e, w0, 0.0) + jnp.where(i1 == lo + e, w1, 0.0)
            a = jax.nn.silu(_dot(hb, wg_ref[e0 + e])) * _dot(hb, wu_ref[e0 + e])
            y = y + _dot((a * ge).astype(BF16), wd_ref[e0 + e])
        out = _rms(x + y, nfin_ref[...])
        for c in range(SUBLANES):
            o_ref[pl.ds(c, rows, stride=SUBLANES), :] = out[:, c * LANES:(c + 1) * LANES]


def _experts(xs, tgrp, tnv, norm_ffn, w_coarse, b_coarse, w_fine, b_fine, w_gate, w_up, w_down, norm_final):
    n_e, d, f = w_gate.shape
    n_tiles = tgrp.shape[0]
    assert n_e == N_EXPERTS
    pad = ROUTER_LANES - N_EXPERT_GROUPS - N_EXPERTS
    wr = jnp.concatenate([w_coarse, w_fine, jnp.zeros((d, pad), F32)], axis=-1).astype(BF16)
    br = jnp.concatenate([b_coarse, b_fine, jnp.zeros((pad,), F32)]).reshape(1, -1).astype(F32)
    row2 = lambda a: a.reshape(1, -1).astype(F32)
    const = lambda shape: pl.BlockSpec(shape, lambda i, tg, nv: (0,) * len(shape))
    anyspec = pl.BlockSpec(memory_space=pl.ANY)
    blk = MOE_ROWS * SUBLANES
    grid_spec = pltpu.PrefetchScalarGridSpec(
        num_scalar_prefetch=2,
        grid=(n_tiles,),
        in_specs=[
            pl.BlockSpec((blk, LANES), lambda i, tg, nv: (i, 0)),
            const((1, d)), const((d, ROUTER_LANES)), const((1, ROUTER_LANES)),
            anyspec, anyspec, anyspec,
            const((1, d)),
        ],
        out_specs=pl.BlockSpec((blk, LANES), lambda i, tg, nv: (i, 0)),
        scratch_shapes=[
            pltpu.VMEM((n_e, d, f), BF16),
            pltpu.VMEM((n_e, d, f), BF16),
            pltpu.VMEM((n_e, f, d), BF16),
            pltpu.SemaphoreType.DMA((3,)),
        ],
    )
    return pl.pallas_call(
        _experts_kernel,
        grid_spec=grid_spec,
        out_shape=jax.ShapeDtypeStruct(xs.shape, F32),
        compiler_params=pltpu.CompilerParams(
            dimension_semantics=("arbitrary",), vmem_limit_bytes=VMEM_LIMIT),
        name="experts",
    )(tgrp, tnv, xs, row2(norm_ffn), wr, br,
      w_gate.astype(BF16), w_up.astype(BF16), w_down.astype(BF16), row2(norm_final))


def _unsort_kernel(tab_ref, code_hbm, ys_hbm, o_ref, idx, buf, isem, gsem):
    i = pl.program_id(0)
    nt = pl.num_programs(0)
    slot = lax.rem(i, 2)
    rows = o_ref.shape[0]
    ng = N_EXPERT_GROUPS

    def idx_copy(tile, s3):
        return pltpu.make_async_copy(code_hbm.at[tile], idx.at[pl.ds(s3 * rows, rows)], isem.at[s3])

    def gather_start(tile, s3, bslot):
        def body(r, c):
            cd = idx[s3 * rows + r]
            rk = cd & (TIME_TILE - 1)
            e = (tile * ng + (cd >> TIME_TILE_LOG2)) * 3
            src = jnp.where(rk < tab_ref[e + 2], tab_ref[e], tab_ref[e + 1]) + rk
            pltpu.make_async_copy(ys_hbm.at[pl.ds(pl.multiple_of(src * SUBLANES, SUBLANES), SUBLANES)],
                                  buf.at[bslot, pl.ds(pl.multiple_of(r * SUBLANES, SUBLANES), SUBLANES)],
                                  gsem.at[bslot]).start()
            return c

        lax.fori_loop(0, rows, body, 0, unroll=8)

    @pl.when(i == 0)
    def _():
        idx_copy(0, 0).start()
        idx_copy(0, 0).wait()

        @pl.when(nt > 1)
        def _():
            idx_copy(1, 1).start()
            idx_copy(1, 1).wait()

        gather_start(0, 0, 0)

    @pl.when(i + 2 < nt)
    def _():
        idx_copy(i + 2, lax.rem(i + 2, 3)).start()

    @pl.when((i >= 1) & (i + 1 < nt))
    def _():
        idx_copy(i + 1, lax.rem(i + 1, 3)).wait()

    @pl.when(i + 1 < nt)
    def _():
        gather_start(i + 1, lax.rem(i + 1, 3), 1 - slot)

    pltpu.make_async_copy(ys_hbm.at[pl.ds(0, rows * SUBLANES)], buf.at[slot], gsem.at[slot]).wait()
    o_ref[...] = jnp.concatenate(
        [buf[slot, pl.ds(c, rows, stride=SUBLANES), :] for c in range(SUBLANES)], axis=-1)


def _unsort(ys, code, base, n_tok, d):
    rows = TIME_TILE
    n_tiles = n_tok // rows
    anyspec = pl.BlockSpec(memory_space=pl.ANY)
    grid_spec = pltpu.PrefetchScalarGridSpec(
        num_scalar_prefetch=1,
        grid=(n_tiles,),
        in_specs=[anyspec, anyspec],
        out_specs=pl.BlockSpec((rows, d), lambda i, base: (i, 0)),
        scratch_shapes=[
            pltpu.SMEM((3 * rows,), I32),
            pltpu.VMEM((2, rows * SUBLANES, LANES), F32),
            pltpu.SemaphoreType.DMA((3,)),
            pltpu.SemaphoreType.DMA((2,)),
        ],
    )
    return pl.pallas_call(
        _unsort_kernel,
        grid_spec=grid_spec,
        out_shape=jax.ShapeDtypeStruct((n_tok, d), F32),
        compiler_params=pltpu.CompilerParams(
            dimension_semantics=("arbitrary",), vmem_limit_bytes=VMEM_LIMIT),
        name="unsort",
    )(base, code.reshape(n_tiles, rows), ys)


def kernel(x, norm_mix, w_in, pool_w, pool_scale, ssm_a_re, ssm_a_im, ssm_log_step, ssm_b_re, ssm_b_im, ssm_c_re, ssm_c_im, ssm_d, glu_w, glu_b, w_out, norm_ffn, router_coarse_w, router_coarse_b, router_fine_w, router_fine_b, exp_w_gate, exp_w_up, exp_w_down, norm_final):
    assert norm_mix.shape[0] == 1, "the experts call fuses the final norm: single-layer blocks only"
    b, l, d = x.shape
    t = b * l
    xs, code, tab, tgrp, tnv = _mixer(
        x, norm_mix[0], w_in[0], pool_w[0], pool_scale[0], ssm_a_re[0], ssm_a_im[0], ssm_log_step[0],
        ssm_b_re[0], ssm_b_im[0], ssm_c_re[0], ssm_c_im[0], ssm_d[0], glu_w[0], glu_b[0], w_out[0],
        norm_ffn[0], router_coarse_w[0], router_coarse_b[0])
    ys = _experts(xs, tgrp, tnv, norm_ffn[0], router_coarse_w[0], router_coarse_b[0],
                  router_fine_w[0], router_fine_b[0], exp_w_gate[0], exp_w_up[0], exp_w_down[0], norm_final)
    y = _unsort(ys, code, tab, t, d)
    return y.reshape(b, l, d)
```

```python
import functools

import jax
import jax.numpy as jnp
from jax import lax
from jax.experimental import pallas as pl
from jax.experimental.pallas import tpu as pltpu

EPS = 1e-6
POOL_WINDOWS = (2, 4, 8, 16)
POOL_GROUP = 128
POOL_TAIL = 16
SSM_GROUP = 16
SSM_STATE = 64
N_EXPERT_GROUPS = 4
EXPERTS_PER_GROUP = 4
N_EXPERTS = 16
PAIRS_PER_GROUP = 6
N_CLASSES = N_EXPERT_GROUPS * PAIRS_PER_GROUP

SUBLANES = 8
LANES = 128
SLAB_GROUPS = 8
STATE_SLAB = SLAB_GROUPS * SSM_STATE
TIME_TILE = 256
TILE_ROWS = 256
TILE_ROWS_LOG2 = 8
ROUTER_LANES = 128
VMEM_LIMIT = 56 * 1024 * 1024
NEG = -3.0e38

F32 = jnp.float32
BF16 = jnp.bfloat16
I32 = jnp.int32
NT_DIMS = (((1,), (1,)), ((), ()))


def _dot(a, b):
    return jnp.dot(a, b, preferred_element_type=F32)


def _dot_nt(a, b):
    return lax.dot_general(a, b, NT_DIMS, preferred_element_type=F32)


def _rms(x, g):
    return x * lax.rsqrt(jnp.mean(x * x, axis=-1, keepdims=True) + EPS) * g


def _rows8(*rows):
    r8 = lax.broadcasted_iota(I32, (SUBLANES, LANES), 0)
    out = jnp.zeros((SUBLANES, LANES), F32)
    for k, row in enumerate(rows):
        out = jnp.where(r8 == k, jnp.broadcast_to(row, (SUBLANES, LANES)), out)
    return out


def _route(logits, lane):
    is_c = lane < N_EXPERT_GROUPS
    cm = jnp.where(is_c, logits, NEG)
    cmax = jnp.max(cm, axis=-1, keepdims=True)
    gi = jnp.min(jnp.where(cm == cmax, lane, ROUTER_LANES), axis=-1, keepdims=True)
    lo = N_EXPERT_GROUPS + gi * EXPERTS_PER_GROUP
    insel = (lane >= lo) & (lane < lo + EXPERTS_PER_GROUP)
    fs = jnp.where(insel, logits, NEG)
    v0 = jnp.max(fs, axis=-1, keepdims=True)
    i0 = jnp.min(jnp.where(insel & (fs == v0), lane, ROUTER_LANES), axis=-1, keepdims=True)
    rest = insel & (lane != i0)
    fs2 = jnp.where(rest, logits, NEG)
    v1 = jnp.max(fs2, axis=-1, keepdims=True)
    i1 = jnp.min(jnp.where(rest & (fs2 == v1), lane, ROUTER_LANES), axis=-1, keepdims=True)
    return gi, lo, i0, i1


def _mixer_kernel(x_ref, nm_ref, win_ref, pw_ref, ps_ref, perm_ref, permt_ref, wb_ref,
                  are_ref, aim_ref, apre_ref, apim_ref, mkre_ref, mkim_ref, asre_ref, asim_ref,
                  wcre_ref, wcim_ref, d_ref, gw_ref, gb_ref, wout_ref, nf_ref, wr_ref, br_ref, tri_ref,
                  xs_hbm, dst_ref, alloc_ref, fin_ref,
                  zpad, bur, bui, st_r, st_i, x1t, meta_v, meta_s, alloc_v, ssem, msem, zsem,
                  *, tc, d_pool, d_ssm, n_tiles):
    bq = pl.program_id(0)
    n = pl.program_id(1)
    n_t = pl.num_programs(1)
    step = bq * n_t + n
    n_steps = pl.num_programs(0) * n_t
    slot = lax.rem(step, 2)
    seg = tc // SUBLANES
    n_slab = d_ssm // LANES
    lane1 = lax.broadcasted_iota(I32, (1, LANES), 1)

    @pl.when(n == 0)
    def _():
        zpad[0:POOL_TAIL, :] = jnp.zeros((POOL_TAIL, d_pool), F32)
        st_r[...] = jnp.zeros_like(st_r)
        st_i[...] = jnp.zeros_like(st_i)

    @pl.when(step == 0)
    def _():
        lane_f = lane1.astype(F32)
        alloc_v[...] = _rows8(jnp.where(lane1 < N_CLASSES, lane_f, 0.0), jnp.zeros((1, LANES), F32),
                              jnp.full((1, LANES), float(N_CLASSES), F32))

    x = x_ref[0]
    hn = _rms(x, nm_ref[...]).astype(BF16)
    z = _dot(hn, win_ref[...])

    zp = z[:, :d_pool]
    zpad[POOL_TAIL:POOL_TAIL + tc, :] = zp
    t = n * tc + lax.broadcasted_iota(I32, (tc, 1), 0)
    pooled = []
    for gi, w in enumerate(POOL_WINDOWS):
        cols = slice(gi * POOL_GROUP, (gi + 1) * POOL_GROUP)
        acc = zp[:, cols]
        for k in range(1, w):
            acc = acc + zpad[POOL_TAIL - k:POOL_TAIL - k + tc, cols]
        cnt = jnp.minimum(t + 1, w).astype(F32)
        p = acc / cnt - zp[:, cols]
        pooled.append(_dot(p.astype(BF16), pw_ref[gi]))
    y_pool = jnp.concatenate(pooled, axis=-1) * ps_ref[...]
    zpad[0:POOL_TAIL, :] = zpad[tc:tc + POOL_TAIL, :]

    u = z[:, d_pool:]
    up = _dot(perm_ref[...], u.astype(BF16))
    upb = up.astype(BF16)
    for m in range(n_slab):
        r = _dot(upb[:, m * LANES:(m + 1) * LANES], wb_ref[m])
        bur[:, m * STATE_SLAB:(m + 1) * STATE_SLAB] = r[:, :STATE_SLAB]
        bui[:, m * STATE_SLAB:(m + 1) * STATE_SLAB] = r[:, STATE_SLAB:]

    row = lax.broadcasted_iota(I32, (SUBLANES, STATE_SLAB), 0)
    for c in range(n_slab):
        cs = slice(c * STATE_SLAB, (c + 1) * STATE_SLAB)
        ar = are_ref[:, cs]
        ai = aim_ref[:, cs]

        def scan_body(s, carry, cs=cs, ar=ar, ai=ai):
            xr, xi = carry
            r0 = pl.multiple_of(s * SUBLANES, SUBLANES)
            nxr = ar * xr - ai * xi + bur[pl.ds(r0, SUBLANES), cs]
            nxi = ar * xi + ai * xr + bui[pl.ds(r0, SUBLANES), cs]
            bur[pl.ds(r0, SUBLANES), cs] = nxr
            bui[pl.ds(r0, SUBLANES), cs] = nxi
            return nxr, nxi

        zero = jnp.zeros((SUBLANES, STATE_SLAB), F32)
        er, ei = lax.fori_loop(0, seg, scan_body, (zero, zero), unroll=4)

        fr = jnp.where(row == 0, st_r[:, cs], pltpu.roll(er, 1, 0))
        fi = jnp.where(row == 0, st_i[:, cs], pltpu.roll(ei, 1, 0))
        for ki, k in enumerate((1, 2, 4)):
            mr = mkre_ref[ki, :, cs]
            mi = mkim_ref[ki, :, cs]
            rr = pltpu.roll(fr, k, 0)
            ri = pltpu.roll(fi, k, 0)
            fr, fi = fr + mr * rr - mi * ri, fi + mr * ri + mi * rr
        asr = asre_ref[:, cs]
        asi = asim_ref[:, cs]
        nsr = asr * fr - asi * fi + er
        nsi = asr * fi + asi * fr + ei
        st_r[:, cs] = pltpu.roll(nsr, 1, 0)
        st_i[:, cs] = pltpu.roll(nsi, 1, 0)

        def fix_body(s, carry, cs=cs, fr=fr, fi=fi):
            r0 = pl.multiple_of(s * SUBLANES, SUBLANES)
            pr = apre_ref[pl.ds(r0, SUBLANES), cs]
            pi = apim_ref[pl.ds(r0, SUBLANES), cs]
            bur[pl.ds(r0, SUBLANES), cs] = bur[pl.ds(r0, SUBLANES), cs] + pr * fr - pi * fi
            bui[pl.ds(r0, SUBLANES), cs] = bui[pl.ds(r0, SUBLANES), cs] + pr * fi + pi * fr
            return carry

        lax.fori_loop(0, seg, fix_body, 0, unroll=4)

    ys = []
    for m in range(n_slab):
        cs = slice(m * STATE_SLAB, (m + 1) * STATE_SLAB)
        ys.append(_dot(bur[:, cs].astype(BF16), wcre_ref[m]) + _dot(bui[:, cs].astype(BF16), wcim_ref[m]))
    y = jnp.concatenate(ys, axis=-1) + d_ref[...] * up
    y = jax.nn.gelu(y)
    y = y * jax.nn.sigmoid(_dot(y.astype(BF16), gw_ref[...]) + gb_ref[...])
    y_ssm = _dot(permt_ref[...], y.astype(BF16))

    mix = _dot(y_pool.astype(BF16), wout_ref[0:d_pool, :]) + _dot(y_ssm.astype(BF16), wout_ref[d_pool:, :])
    x1 = x + mix

    hb = _rms(x1, nf_ref[...]).astype(BF16)
    logits = _dot(hb, wr_ref[...]) + br_ref[...]
    lane = lax.broadcasted_iota(I32, (tc, ROUTER_LANES), 1)
    gi, lo, i0, i1 = _route(logits, lane)
    ea = jnp.minimum(i0, i1) - lo
    eb = jnp.maximum(i0, i1) - lo
    cls = gi * PAIRS_PER_GROUP + ((ea * (7 - ea)) >> 1) + (eb - ea - 1)
    oh = (lane == cls).astype(F32)
    ohb = oh.astype(BF16)
    before = _dot(tri_ref[...], ohb)
    ones8 = jnp.ones((SUBLANES, ROUTER_LANES), BF16)
    rank = _dot_nt(ones8, (before * oh).astype(BF16))[0:1, :]
    cnt = jnp.sum(oh, axis=0, keepdims=True)

    st = alloc_v[...]
    open_t, fill, free = st[0:1], st[1:2], st[2:3]
    room = float(TILE_ROWS) - fill
    spill = cnt > room
    spill_f = spill.astype(F32)
    upper = (lax.broadcasted_iota(I32, (LANES, LANES), 0) < lax.broadcasted_iota(I32, (LANES, LANES), 1))
    earlier = _dot(_rows8(spill_f).astype(BF16), upper.astype(F32).astype(BF16))[0:1]
    new_t = free + earlier
    per_row = _dot_nt(_rows8(open_t, new_t, fill).astype(BF16), ohb)
    open_r, new_r, fill_r = per_row[0:1], per_row[1:2], per_row[2:3]
    room_r = float(TILE_ROWS) - fill_r
    dst = jnp.where(rank < room_r, open_r * float(TILE_ROWS) + fill_r + rank,
                    new_r * float(TILE_ROWS) + rank - room_r).astype(I32)
    dst_ref[0] = dst
    alloc_ref[0] = jnp.where(spill, new_t, -1.0).astype(I32)
    new_state = _rows8(jnp.where(spill, new_t, open_t), jnp.where(spill, cnt - room, fill + cnt),
                       free + jnp.sum(spill_f, axis=-1, keepdims=True))
    alloc_v[...] = new_state
    fin_ref[...] = new_state.astype(I32)

    full = pltpu.make_async_copy(x1t.at[slot], xs_hbm.at[pl.ds(0, tc * SUBLANES)], ssem.at[slot])

    @pl.when(step >= 2)
    def _():
        full.wait()

    meta_v[0:1, :] = dst
    meta_cp = pltpu.make_async_copy(meta_v.at[0], meta_s.at[pl.ds(0, tc)], msem)
    meta_cp.start()
    for c in range(SUBLANES):
        x1t[slot, pl.ds(c, tc, stride=SUBLANES), :] = x1[:, c * LANES:(c + 1) * LANES]
    meta_cp.wait()

    def send(r2, carry):
        for prio in range(2):
            r = r2 * 2 + prio
            d = meta_s[r]
            pltpu.make_async_copy(x1t.at[slot, pl.ds(pl.multiple_of(r * SUBLANES, SUBLANES), SUBLANES)],
                                  xs_hbm.at[pl.ds(pl.multiple_of(d * SUBLANES, SUBLANES), SUBLANES)],
                                  ssem.at[slot]).start(priority=prio)
        return carry

    lax.fori_loop(0, tc // 2, send, 0, unroll=4)

    @pl.when(step == n_steps - 1)
    def _():
        @pl.when(step >= 1)
        def _():
            pltpu.make_async_copy(x1t.at[1 - slot], xs_hbm.at[pl.ds(0, tc * SUBLANES)], ssem.at[1 - slot]).wait()

        zsrc = x1t.at[1 - slot]
        zsrc[...] = jnp.zeros((tc * SUBLANES, LANES), F32)
        meta_v[:, 0:LANES] = new_state.astype(I32)
        for k in range(3):
            cp = pltpu.make_async_copy(meta_v.at[k, pl.ds(0, LANES)], meta_s.at[pl.ds(k * LANES, LANES)], msem)
            cp.start()
            cp.wait()

        def zcopy(first_row, n_rows):
            return pltpu.make_async_copy(
                zsrc.at[pl.ds(0, n_rows * SUBLANES)],
                xs_hbm.at[pl.ds(pl.multiple_of(first_row * SUBLANES, SUBLANES), n_rows * SUBLANES)], zsem)

        for c in range(N_CLASSES):
            tile = meta_s[c]
            fill_c = meta_s[LANES + c]
            rem = TILE_ROWS - fill_c
            for bit in range(TILE_ROWS_LOG2, -1, -1):
                size = 1 << bit
                first = (tile << TILE_ROWS_LOG2) + fill_c + (rem & ~(2 * size - 1))

                @pl.when((rem & size) != 0)
                def _(first=first, size=size):
                    cp = zcopy(first, size)
                    cp.start()
                    cp.wait()

        def tail(k, carry):
            cp = zcopy(k << TILE_ROWS_LOG2, TILE_ROWS)
            cp.start()
            cp.wait()
            return carry

        lax.fori_loop(meta_s[2 * LANES], n_tiles, tail, 0)
        full.wait()


def _ssm_tables(a_re, a_im, log_step, b_re, b_im, c_re, c_im, seg):
    g, p = a_re.shape
    h = b_re.shape[-1]
    n_slab = g // SLAB_GROUPS
    lr, li = a_re.astype(F32), a_im.astype(F32)
    step = jnp.exp(log_step.astype(F32))[:, None]
    mag = jnp.exp(lr * step)
    ab_re = mag * jnp.cos(li * step)
    ab_im = mag * jnp.sin(li * step)
    den = lr * lr + li * li
    nr, ni = ab_re - 1.0, ab_im
    q_re = (nr * lr + ni * li) / den
    q_im = (ni * lr - nr * li) / den
    br, bi = b_re.astype(F32), b_im.astype(F32)
    bb_re = q_re[..., None] * br - q_im[..., None] * bi
    bb_im = q_re[..., None] * bi + q_im[..., None] * br

    eye = jnp.eye(SLAB_GROUPS, dtype=F32)

    def b_slab(bb):
        bb = bb.reshape(n_slab, SLAB_GROUPS, p, h)
        w = jnp.einsum('mgph,gk->mghkp', bb, eye)
        return w.reshape(n_slab, SLAB_GROUPS * h, SLAB_GROUPS * p)

    wb = jnp.concatenate([b_slab(bb_re), b_slab(bb_im)], axis=-1).astype(BF16)

    def c_slab(cc):
        cc = cc.astype(F32).reshape(n_slab, SLAB_GROUPS, h, p)
        w = jnp.einsum('mghp,gk->mgpkh', cc, eye)
        return w.reshape(n_slab, SLAB_GROUPS * p, SLAB_GROUPS * h)

    wc_re = c_slab(c_re).astype(BF16)
    wc_im = c_slab(-c_im).astype(BF16)

    def apow(nn):
        nn = nn.astype(F32)[:, None, None]
        m_ = jnp.exp(nn * (lr * step))
        ang = nn * (li * step)
        return (m_ * jnp.cos(ang)).reshape(-1, g * p), (m_ * jnp.sin(ang)).reshape(-1, g * p)

    a1r, a1i = apow(jnp.ones((1,), F32))
    a_r = jnp.broadcast_to(a1r, (SUBLANES, g * p))
    a_i = jnp.broadcast_to(a1i, (SUBLANES, g * p))
    pr, pi = apow(jnp.arange(1, seg + 1))
    ap_r = jnp.repeat(pr, SUBLANES, axis=0)
    ap_i = jnp.repeat(pi, SUBLANES, axis=0)
    ks = jnp.array([1, 2, 4])
    mr, mi = apow(ks * seg)
    rows = jnp.arange(SUBLANES)[None, :, None]
    mask = (rows >= ks[:, None, None]).astype(F32)
    mk_r = mr[:, None, :] * mask
    mk_i = mi[:, None, :] * mask
    sr, si = apow(jnp.full((1,), seg, F32))
    as_r = jnp.broadcast_to(sr, (SUBLANES, g * p))
    as_i = jnp.broadcast_to(si, (SUBLANES, g * p))
    return wb, a_r, a_i, ap_r, ap_i, mk_r, mk_i, as_r, as_i, wc_re, wc_im


def _router_weights(w_coarse, b_coarse, w_fine, b_fine):
    d = w_coarse.shape[0]
    pad = ROUTER_LANES - N_EXPERT_GROUPS - N_EXPERTS
    wr = jnp.concatenate([w_coarse, w_fine, jnp.zeros((d, pad), F32)], axis=-1).astype(BF16)
    br = jnp.concatenate([b_coarse, b_fine, jnp.zeros((pad,), F32)]).reshape(1, -1).astype(F32)
    return wr, br


def _mixer(x, norm_mix, w_in, pool_w, pool_scale, a_re, a_im, log_step, b_re, b_im, c_re, c_im,
           d_skip, glu_w, glu_b, w_out, norm_ffn, wr, br):
    b, l, d = x.shape
    d_pool = pool_scale.shape[-1]
    d_ssm = d_skip.shape[-1]
    tc = TIME_TILE
    assert l % tc == 0 and d_pool == len(POOL_WINDOWS) * POOL_GROUP and d_ssm % LANES == 0
    assert d == SUBLANES * LANES, "one token row must fill exactly one (8, 128) tile"
    assert tc == TILE_ROWS, "a mixer tile must never need more than one new sorted tile per class"
    assert tc <= 3 * LANES and N_CLASSES <= LANES
    seg = tc // SUBLANES
    n_state = a_re.shape[0] * a_re.shape[1]
    tables = _ssm_tables(a_re, a_im, log_step, b_re, b_im, c_re, c_im, seg)
    wb, a_r, a_i, ap_r, ap_i, mk_r, mk_i, as_r, as_i, wc_re, wc_im = tables

    i = jnp.arange(tc)
    src = (i % SUBLANES) * seg + i // SUBLANES
    perm = (src[:, None] == i[None, :]).astype(BF16)
    permt = perm.T
    tri = (i[None, :] < i[:, None]).astype(BF16)

    row2 = lambda a: a.reshape(1, -1).astype(F32)
    operands = [
        x, row2(norm_mix), w_in.astype(BF16), pool_w.astype(BF16), row2(pool_scale), perm, permt, wb,
        a_r, a_i, ap_r, ap_i, mk_r, mk_i, as_r, as_i, wc_re, wc_im,
        row2(d_skip), glu_w.astype(BF16), row2(glu_b), w_out.astype(BF16),
        row2(norm_ffn), wr, br, tri,
    ]

    def whole(a):
        nd = a.ndim
        return pl.BlockSpec(a.shape, lambda bi, ni, nd=nd: (0,) * nd)

    n_t = l // tc
    n_tiles = (b * l) // TILE_ROWS + N_CLASSES
    in_specs = [pl.BlockSpec((1, tc, d), lambda bi, ni: (bi, ni, 0))] + [whole(a) for a in operands[1:]]
    kern = functools.partial(_mixer_kernel, tc=tc, d_pool=d_pool, d_ssm=d_ssm, n_tiles=n_tiles)
    return pl.pallas_call(
        kern,
        grid=(b, n_t),
        in_specs=in_specs,
        out_specs=[
            pl.BlockSpec(memory_space=pl.ANY),
            pl.BlockSpec((1, 1, tc), lambda bi, ni: (bi * n_t + ni, 0, 0)),
            pl.BlockSpec((1, 1, LANES), lambda bi, ni: (bi * n_t + ni, 0, 0)),
            pl.BlockSpec((SUBLANES, LANES), lambda bi, ni: (0, 0)),
        ],
        out_shape=[
            jax.ShapeDtypeStruct((n_tiles * TILE_ROWS * SUBLANES, LANES), F32),
            jax.ShapeDtypeStruct((b * n_t, 1, tc), I32),
            jax.ShapeDtypeStruct((b * n_t, 1, LANES), I32),
            jax.ShapeDtypeStruct((SUBLANES, LANES), I32),
        ],
        scratch_shapes=[
            pltpu.VMEM((POOL_TAIL + tc, d_pool), F32),
            pltpu.VMEM((tc, n_state), F32),
            pltpu.VMEM((tc, n_state), F32),
            pltpu.VMEM((SUBLANES, n_state), F32),
            pltpu.VMEM((SUBLANES, n_state), F32),
            pltpu.VMEM((2, tc * SUBLANES, LANES), F32),
            pltpu.VMEM((SUBLANES, tc), I32),
            pltpu.SMEM((3 * LANES,), I32),
            pltpu.VMEM((SUBLANES, LANES), F32),
            pltpu.SemaphoreType.DMA((2,)),
            pltpu.SemaphoreType.DMA,
            pltpu.SemaphoreType.DMA,
        ],
        compiler_params=pltpu.CompilerParams(
            dimension_semantics=("arbitrary", "arbitrary"), vmem_limit_bytes=VMEM_LIMIT),
        name="mixer",
    )(*operands)


def _experts_kernel(alloc_ref, fin_ref, x_ref, nf_ref, wr_ref, br_ref, wg_hbm, wu_hbm, wd_hbm,
                    nfin_ref, o_ref, wg_ref, wu_ref, wd_ref, tcls, tnv, wsem, *, n_steps):
    i = pl.program_id(0)
    n_tiles = pl.num_programs(0)
    rows = TILE_ROWS

    @pl.when(i == 0)
    def _():
        copies = [pltpu.make_async_copy(src, dst, wsem.at[j])
                  for j, (src, dst) in enumerate(((wg_hbm, wg_ref), (wu_hbm, wu_ref), (wd_hbm, wd_ref)))]
        for cp in copies:
            cp.start()
        free = fin_ref[2 * LANES]

        def init(k, carry):
            tcls[k] = jnp.minimum(k, N_CLASSES - 1)
            tnv[k] = jnp.where(k < free, rows, 0)
            return carry

        lax.fori_loop(0, n_tiles, init, 0)

        def log(s, carry):
            for c in range(N_CLASSES):
                t = alloc_ref[s * LANES + c]
                tcls[jnp.where(t >= 0, t, n_tiles)] = c
            return carry

        lax.fori_loop(0, n_steps, log, 0)
        for c in range(N_CLASSES):
            tnv[fin_ref[c]] = fin_ref[LANES + c]
        for cp in copies:
            cp.wait()

    @pl.when(tnv[i] == 0)
    def _():
        o_ref[...] = jnp.zeros_like(o_ref)

    @pl.when(tnv[i] > 0)
    def _():
        c = tcls[i]
        g = (c >= PAIRS_PER_GROUP).astype(I32) + (c >= 2 * PAIRS_PER_GROUP).astype(I32) \
            + (c >= 3 * PAIRS_PER_GROUP).astype(I32)
        p = c - g * PAIRS_PER_GROUP
        a = (p >= 3).astype(I32) + (p >= 5).astype(I32)
        b = p - ((a * (7 - a)) >> 1) + a + 1
        ea = g * EXPERTS_PER_GROUP + a
        eb = g * EXPERTS_PER_GROUP + b
        x = jnp.concatenate([x_ref[pl.ds(k, rows, stride=SUBLANES), :] for k in range(SUBLANES)], axis=-1)
        hb = _rms(x, nf_ref[...]).astype(BF16)
        logits = _dot(hb, wr_ref[...]) + br_ref[...]
        lane = lax.broadcasted_iota(I32, (rows, ROUTER_LANES), 1)
        is_c = lane < N_EXPERT_GROUPS
        cmax = jnp.max(jnp.where(is_c, logits, NEG), axis=-1, keepdims=True)
        psum = jnp.sum(jnp.where(is_c, jnp.exp(logits - cmax), 0.0), axis=-1, keepdims=True)
        cg = jnp.sum(jnp.where(lane == g, logits, 0.0), axis=-1, keepdims=True)
        p_g = jnp.exp(cg - cmax) / psum
        va = jnp.sum(jnp.where(lane == N_EXPERT_GROUPS + ea, logits, 0.0), axis=-1, keepdims=True)
        vb = jnp.sum(jnp.where(lane == N_EXPERT_GROUPS + eb, logits, 0.0), axis=-1, keepdims=True)
        vmax = jnp.maximum(va, vb)
        exa = jnp.exp(va - vmax)
        exb = jnp.exp(vb - vmax)
        den = exa + exb
        y = jnp.zeros_like(x)
        for e, w in ((ea, p_g * (exa / den)), (eb, p_g * (exb / den))):
            act = jax.nn.silu(_dot(hb, wg_ref[e])) * _dot(hb, wu_ref[e])
            y = y + _dot((act * w).astype(BF16), wd_ref[e])
        out = _rms(x + y, nfin_ref[...])
        for k in range(SUBLANES):
            o_ref[pl.ds(k, rows, stride=SUBLANES), :] = out[:, k * LANES:(k + 1) * LANES]


def _experts(xs, alloc, fin, norm_ffn, wr, br, w_gate, w_up, w_down, norm_final):
    n_e, d, f = w_gate.shape
    assert n_e == N_EXPERTS
    blk = TILE_ROWS * SUBLANES
    n_tiles = xs.shape[0] // blk
    n_steps = alloc.shape[0] // LANES
    row2 = lambda a: a.reshape(1, -1).astype(F32)
    const = lambda shape: pl.BlockSpec(shape, lambda i, al, fi: (0,) * len(shape))
    anyspec = pl.BlockSpec(memory_space=pl.ANY)
    grid_spec = pltpu.PrefetchScalarGridSpec(
        num_scalar_prefetch=2,
        grid=(n_tiles,),
        in_specs=[
            pl.BlockSpec((blk, LANES), lambda i, al, fi: (i, 0)),
            const((1, d)), const((d, ROUTER_LANES)), const((1, ROUTER_LANES)),
            anyspec, anyspec, anyspec,
            const((1, d)),
        ],
        out_specs=pl.BlockSpec((blk, LANES), lambda i, al, fi: (i, 0)),
        scratch_shapes=[
            pltpu.VMEM((n_e, d, f), BF16),
            pltpu.VMEM((n_e, d, f), BF16),
            pltpu.VMEM((n_e, f, d), BF16),
            pltpu.SMEM((n_tiles + 1,), I32),
            pltpu.SMEM((n_tiles + 1,), I32),
            pltpu.SemaphoreType.DMA((3,)),
        ],
    )
    return pl.pallas_call(
        functools.partial(_experts_kernel, n_steps=n_steps),
        grid_spec=grid_spec,
        out_shape=jax.ShapeDtypeStruct(xs.shape, F32),
        compiler_params=pltpu.CompilerParams(
            dimension_semantics=("arbitrary",), vmem_limit_bytes=VMEM_LIMIT),
        name="experts",
    )(alloc, fin, xs, row2(norm_ffn), wr, br,
      w_gate.astype(BF16), w_up.astype(BF16), w_down.astype(BF16), row2(norm_final))


def _unsort_kernel(dst_hbm, ys_hbm, o_ref, idx, buf, isem, gsem):
    i = pl.program_id(0)
    nt = pl.num_programs(0)
    slot = lax.rem(i, 2)
    rows = o_ref.shape[0]

    def idx_copy(tile, s3):
        return pltpu.make_async_copy(dst_hbm.at[tile], idx.at[pl.ds(s3 * rows, rows)], isem.at[s3])

    def gather_start(s3, bslot):
        def body(r2, c):
            for prio in range(2):
                r = r2 * 2 + prio
                src = idx[s3 * rows + r]
                pltpu.make_async_copy(ys_hbm.at[pl.ds(pl.multiple_of(src * SUBLANES, SUBLANES), SUBLANES)],
                                      buf.at[bslot, pl.ds(pl.multiple_of(r * SUBLANES, SUBLANES), SUBLANES)],
                                      gsem.at[bslot]).start(priority=prio)
            return c

        lax.fori_loop(0, rows // 2, body, 0, unroll=4)

    @pl.when(i == 0)
    def _():
        idx_copy(0, 0).start()
        idx_copy(0, 0).wait()

        @pl.when(nt > 1)
        def _():
            idx_copy(1, 1).start()
            idx_copy(1, 1).wait()

        gather_start(0, 0)

    @pl.when(i + 2 < nt)
    def _():
        idx_copy(i + 2, lax.rem(i + 2, 3)).start()

    @pl.when((i >= 1) & (i + 1 < nt))
    def _():
        idx_copy(i + 1, lax.rem(i + 1, 3)).wait()

    @pl.when(i + 1 < nt)
    def _():
        gather_start(lax.rem(i + 1, 3), 1 - slot)

    pltpu.make_async_copy(ys_hbm.at[pl.ds(0, rows * SUBLANES)], buf.at[slot], gsem.at[slot]).wait()
    o_ref[...] = jnp.concatenate(
        [buf[slot, pl.ds(c, rows, stride=SUBLANES), :] for c in range(SUBLANES)], axis=-1)


def _unsort(ys, dst, n_tok, d):
    n_tiles, _, rows = dst.shape
    anyspec = pl.BlockSpec(memory_space=pl.ANY)
    return pl.pallas_call(
        _unsort_kernel,
        grid=(n_tiles,),
        in_specs=[anyspec, anyspec],
        out_specs=pl.BlockSpec((rows, d), lambda i: (i, 0)),
        out_shape=jax.ShapeDtypeStruct((n_tok, d), F32),
        scratch_shapes=[
            pltpu.SMEM((3 * rows,), I32),
            pltpu.VMEM((2, rows * SUBLANES, LANES), F32),
            pltpu.SemaphoreType.DMA((3,)),
            pltpu.SemaphoreType.DMA((2,)),
        ],
        compiler_params=pltpu.CompilerParams(
            dimension_semantics=("arbitrary",), vmem_limit_bytes=VMEM_LIMIT),
        name="unsort",
    )(dst.reshape(n_tiles, rows), ys)


def kernel(x, norm_mix, w_in, pool_w, pool_scale, ssm_a_re, ssm_a_im, ssm_log_step, ssm_b_re, ssm_b_im, ssm_c_re, ssm_c_im, ssm_d, glu_w, glu_b, w_out, norm_ffn, router_coarse_w, router_coarse_b, router_fine_w, router_fine_b, exp_w_gate, exp_w_up, exp_w_down, norm_final):
    assert norm_mix.shape[0] == 1, "the experts call fuses the final norm: single-layer blocks only"
    b, l, d = x.shape
    wr, br = _router_weights(router_coarse_w[0], router_coarse_b[0], router_fine_w[0], router_fine_b[0])
    xs, dst, alloc, fin = _mixer(
        x, norm_mix[0], w_in[0], pool_w[0], pool_scale[0], ssm_a_re[0], ssm_a_im[0], ssm_log_step[0],
        ssm_b_re[0], ssm_b_im[0], ssm_c_re[0], ssm_c_im[0], ssm_d[0], glu_w[0], glu_b[0], w_out[0],
        norm_ffn[0], wr, br)
    ys = _experts(xs, alloc.reshape(-1), fin.reshape(-1), norm_ffn[0], wr, br,
                  exp_w_gate[0], exp_w_up[0], exp_w_down[0], norm_final)
    y = _unsort(ys, dst, b * l, d)
    return y.reshape(b, l, d)
```

```python
import functools

import jax
import jax.numpy as jnp
from jax import lax
from jax.experimental import pallas as pl
from jax.experimental.pallas import tpu as pltpu

EPS = 1e-6
POOL_WINDOWS = (2, 4, 8, 16)
POOL_GROUP = 128
POOL_TAIL = 16
POOL_HIST = 32
SSM_GROUP = 16
SSM_STATE = 64
N_EXPERT_GROUPS = 4
EXPERTS_PER_GROUP = 4
N_EXPERTS = 16
PAIRS_PER_GROUP = 6
N_CLASSES = N_EXPERT_GROUPS * PAIRS_PER_GROUP

SUBLANES = 8
LANES = 128
SLAB_GROUPS = 8
STATE_SLAB = SLAB_GROUPS * SSM_STATE
SCAN_LANES = 1024
TIME_TILE = 256
TILE_ROWS = 256
TILE_ROWS_LOG2 = 8
ROUTER_LANES = 128
VMEM_LIMIT = 56 * 1024 * 1024
NEG = -3.0e38

F32 = jnp.float32
BF16 = jnp.bfloat16
I32 = jnp.int32
NT_DIMS = (((1,), (1,)), ((), ()))


def _dot(a, b):
    return jnp.dot(a, b, preferred_element_type=F32)


def _dot_nt(a, b):
    return lax.dot_general(a, b, NT_DIMS, preferred_element_type=F32)


def _rms(x, g):
    return x * lax.rsqrt(jnp.mean(x * x, axis=-1, keepdims=True) + EPS) * g


def _rows8(*rows):
    r8 = lax.broadcasted_iota(I32, (SUBLANES, LANES), 0)
    out = jnp.zeros((SUBLANES, LANES), F32)
    for k, row in enumerate(rows):
        out = jnp.where(r8 == k, jnp.broadcast_to(row, (SUBLANES, LANES)), out)
    return out


def _route(logits, lane):
    is_c = lane < N_EXPERT_GROUPS
    cm = jnp.where(is_c, logits, NEG)
    cmax = jnp.max(cm, axis=-1, keepdims=True)
    gi = jnp.min(jnp.where(cm == cmax, lane, ROUTER_LANES), axis=-1, keepdims=True)
    lo = N_EXPERT_GROUPS + gi * EXPERTS_PER_GROUP
    insel = (lane >= lo) & (lane < lo + EXPERTS_PER_GROUP)
    fs = jnp.where(insel, logits, NEG)
    v0 = jnp.max(fs, axis=-1, keepdims=True)
    i0 = jnp.min(jnp.where(insel & (fs == v0), lane, ROUTER_LANES), axis=-1, keepdims=True)
    rest = insel & (lane != i0)
    fs2 = jnp.where(rest, logits, NEG)
    v1 = jnp.max(fs2, axis=-1, keepdims=True)
    i1 = jnp.min(jnp.where(rest & (fs2 == v1), lane, ROUTER_LANES), axis=-1, keepdims=True)
    return gi, lo, i0, i1


def _mixer_kernel(x_ref, nm_ref, win_ref, pw_ref, ps_ref, perm_ref, permt_ref, wb_ref,
                  are_ref, aim_ref, apre_ref, apim_ref, mkre_ref, mkim_ref, asre_ref, asim_ref,
                  wcre_ref, wcim_ref, d_ref, gw_ref, gb_ref, wout_ref, nf_ref, wr_ref, br_ref, tri_ref,
                  xs_hbm, dst_ref, alloc_ref, fin_ref,
                  zpad, lv2, lv4, lv8, bur, bui, st_r, st_i, x1t, meta_v, meta_s, alloc_v, ssem, msem, zsem,
                  *, tc, d_pool, d_ssm, n_tiles):
    bq = pl.program_id(0)
    n = pl.program_id(1)
    n_t = pl.num_programs(1)
    step = bq * n_t + n
    n_steps = pl.num_programs(0) * n_t
    slot = lax.rem(step, 2)
    seg = tc // SUBLANES
    n_slab = d_ssm // LANES
    lane1 = lax.broadcasted_iota(I32, (1, LANES), 1)

    @pl.when(n == 0)
    def _():
        zpad[0:POOL_HIST, :] = jnp.zeros((POOL_HIST, d_pool), F32)
        st_r[...] = jnp.zeros_like(st_r)
        st_i[...] = jnp.zeros_like(st_i)

    @pl.when(step == 0)
    def _():
        lane_f = lane1.astype(F32)
        alloc_v[...] = _rows8(jnp.where(lane1 < N_CLASSES, lane_f, 0.0), jnp.zeros((1, LANES), F32),
                              jnp.full((1, LANES), float(N_CLASSES), F32))

    meta_cp = pltpu.make_async_copy(meta_v.at[0], meta_s.at[pl.ds(0, tc)], msem)

    def sent(s_):
        return pltpu.make_async_copy(x1t.at[s_], xs_hbm.at[pl.ds(0, tc * SUBLANES)], ssem.at[s_])

    def send_rows(s_):
        def send(r2, carry):
            for prio in range(2):
                r = r2 * 2 + prio
                d = meta_s[r]
                pltpu.make_async_copy(x1t.at[s_, pl.ds(pl.multiple_of(r * SUBLANES, SUBLANES), SUBLANES)],
                                      xs_hbm.at[pl.ds(pl.multiple_of(d * SUBLANES, SUBLANES), SUBLANES)],
                                      ssem.at[s_]).start(priority=prio)
            return carry

        lax.fori_loop(0, tc // 2, send, 0, unroll=4)

    @pl.when(step >= 1)
    def _():
        meta_cp.wait()
        send_rows(1 - slot)

    x = x_ref[0]
    hn = _rms(x, nm_ref[...]).astype(BF16)
    z = _dot(hn, win_ref[...])

    zp = z[:, :d_pool]
    top = POOL_HIST + tc
    zpad[POOL_HIST:top, :] = zp
    g1, g2, g3 = POOL_GROUP, 2 * POOL_GROUP, 3 * POOL_GROUP
    lv2[8:top, :] = zpad[8:top, :] + zpad[7:top - 1, :]
    lv4[16:top, :] = lv2[16:top, g1:] + lv2[14:top - 2, g1:]
    lv8[24:top, :] = lv4[24:top, g1:] + lv4[20:top - 4, g1:]
    wins = (lv2[POOL_HIST:top, 0:g1], lv4[POOL_HIST:top, 0:g1], lv8[POOL_HIST:top, 0:g1],
            lv8[POOL_HIST:top, g1:] + lv8[POOL_HIST - 8:top - 8, g1:])
    t = n * tc + lax.broadcasted_iota(I32, (tc, 1), 0)
    pooled = []
    for gi, w in enumerate(POOL_WINDOWS):
        cols = slice(gi * POOL_GROUP, (gi + 1) * POOL_GROUP)
        cnt = jnp.minimum(t + 1, w).astype(F32)
        p = wins[gi] / cnt - zp[:, cols]
        pooled.append(_dot(p.astype(BF16), pw_ref[gi]))
    y_pool = jnp.concatenate(pooled, axis=-1) * ps_ref[...]
    zpad[POOL_HIST - POOL_TAIL:POOL_HIST, :] = zpad[top - POOL_TAIL:top, :]

    u = z[:, d_pool:]
    up = _dot(perm_ref[...], u.astype(BF16))
    upb = up.astype(BF16)
    for m in range(n_slab):
        r = _dot(upb[:, m * LANES:(m + 1) * LANES], wb_ref[m])
        bur[:, m * STATE_SLAB:(m + 1) * STATE_SLAB] = r[:, :STATE_SLAB]
        bui[:, m * STATE_SLAB:(m + 1) * STATE_SLAB] = r[:, STATE_SLAB:]

    row = lax.broadcasted_iota(I32, (SUBLANES, SCAN_LANES), 0)
    for c in range(n_slab * STATE_SLAB // SCAN_LANES):
        cs = slice(c * SCAN_LANES, (c + 1) * SCAN_LANES)
        ar = are_ref[:, cs]
        ai = aim_ref[:, cs]

        def scan_body(s, carry, cs=cs, ar=ar, ai=ai):
            xr, xi = carry
            r0 = pl.multiple_of(s * SUBLANES, SUBLANES)
            nxr = ar * xr - ai * xi + bur[pl.ds(r0, SUBLANES), cs]
            nxi = ar * xi + ai * xr + bui[pl.ds(r0, SUBLANES), cs]
            bur[pl.ds(r0, SUBLANES), cs] = nxr
            bui[pl.ds(r0, SUBLANES), cs] = nxi
            return nxr, nxi

        zero = jnp.zeros((SUBLANES, SCAN_LANES), F32)
        er, ei = lax.fori_loop(0, seg, scan_body, (zero, zero), unroll=4)

        fr = jnp.where(row == 0, st_r[:, cs], pltpu.roll(er, 1, 0))
        fi = jnp.where(row == 0, st_i[:, cs], pltpu.roll(ei, 1, 0))
        for ki, k in enumerate((1, 2, 4)):
            mr = mkre_ref[ki, :, cs]
            mi = mkim_ref[ki, :, cs]
            rr = pltpu.roll(fr, k, 0)
            ri = pltpu.roll(fi, k, 0)
            fr, fi = fr + mr * rr - mi * ri, fi + mr * ri + mi * rr
        asr = asre_ref[:, cs]
        asi = asim_ref[:, cs]
        nsr = asr * fr - asi * fi + er
        nsi = asr * fi + asi * fr + ei
        st_r[:, cs] = pltpu.roll(nsr, 1, 0)
        st_i[:, cs] = pltpu.roll(nsi, 1, 0)

        def fix_body(s, carry, cs=cs, fr=fr, fi=fi):
            r0 = pl.multiple_of(s * SUBLANES, SUBLANES)
            pr = apre_ref[pl.ds(r0, SUBLANES), cs]
            pi = apim_ref[pl.ds(r0, SUBLANES), cs]
            bur[pl.ds(r0, SUBLANES), cs] = bur[pl.ds(r0, SUBLANES), cs] + pr * fr - pi * fi
            bui[pl.ds(r0, SUBLANES), cs] = bui[pl.ds(r0, SUBLANES), cs] + pr * fi + pi * fr
            return carry

        lax.fori_loop(0, seg, fix_body, 0, unroll=4)

    ys = []
    for m in range(n_slab):
        cs = slice(m * STATE_SLAB, (m + 1) * STATE_SLAB)
        ys.append(_dot(bur[:, cs].astype(BF16), wcre_ref[m]) + _dot(bui[:, cs].astype(BF16), wcim_ref[m]))
    y = jnp.concatenate(ys, axis=-1) + d_ref[...] * up
    y = jax.nn.gelu(y)
    y = y * jax.nn.sigmoid(_dot(y.astype(BF16), gw_ref[...]) + gb_ref[...])
    y_ssm = _dot(permt_ref[...], y.astype(BF16))

    mix = _dot(y_pool.astype(BF16), wout_ref[0:d_pool, :]) + _dot(y_ssm.astype(BF16), wout_ref[d_pool:, :])
    x1 = x + mix

    hb = _rms(x1, nf_ref[...]).astype(BF16)
    logits = _dot(hb, wr_ref[...]) + br_ref[...]
    lane = lax.broadcasted_iota(I32, (tc, ROUTER_LANES), 1)
    gi, lo, i0, i1 = _route(logits, lane)
    ea = jnp.minimum(i0, i1) - lo
    eb = jnp.maximum(i0, i1) - lo
    cls = gi * PAIRS_PER_GROUP + ((ea * (7 - ea)) >> 1) + (eb - ea - 1)
    oh = (lane == cls).astype(F32)
    ohb = oh.astype(BF16)
    before = _dot(tri_ref[...], ohb)
    ones8 = jnp.ones((SUBLANES, ROUTER_LANES), BF16)
    rank = _dot_nt(ones8, (before * oh).astype(BF16))[0:1, :]
    cnt = jnp.sum(oh, axis=0, keepdims=True)

    st = alloc_v[...]
    open_t, fill, free = st[0:1], st[1:2], st[2:3]
    room = float(TILE_ROWS) - fill
    spill = cnt > room
    spill_f = spill.astype(F32)
    upper = (lax.broadcasted_iota(I32, (LANES, LANES), 0) < lax.broadcasted_iota(I32, (LANES, LANES), 1))
    earlier = _dot(_rows8(spill_f).astype(BF16), upper.astype(F32).astype(BF16))[0:1]
    new_t = free + earlier
    per_row = _dot_nt(_rows8(open_t, new_t, fill).astype(BF16), ohb)
    open_r, new_r, fill_r = per_row[0:1], per_row[1:2], per_row[2:3]
    room_r = float(TILE_ROWS) - fill_r
    dst = jnp.where(rank < room_r, open_r * float(TILE_ROWS) + fill_r + rank,
                    new_r * float(TILE_ROWS) + rank - room_r).astype(I32)
    dst_ref[0] = dst
    alloc_ref[0] = jnp.where(spill, new_t, -1.0).astype(I32)
    new_state = _rows8(jnp.where(spill, new_t, open_t), jnp.where(spill, cnt - room, fill + cnt),
                       free + jnp.sum(spill_f, axis=-1, keepdims=True))
    alloc_v[...] = new_state
    fin_ref[...] = new_state.astype(I32)

    @pl.when(step >= 2)
    def _():
        sent(slot).wait()

    for c in range(SUBLANES):
        x1t[slot, pl.ds(c, tc, stride=SUBLANES), :] = x1[:, c * LANES:(c + 1) * LANES]
    meta_v[0:1, :] = dst
    meta_cp.start()

    @pl.when(step == n_steps - 1)
    def _():
        meta_cp.wait()
        send_rows(slot)

        @pl.when(step >= 1)
        def _():
            sent(1 - slot).wait()


        zsrc = x1t.at[1 - slot]
        zsrc[...] = jnp.zeros((tc * SUBLANES, LANES), F32)
        meta_v[:, 0:LANES] = new_state.astype(I32)
        for k in range(3):
            cp = pltpu.make_async_copy(meta_v.at[k, pl.ds(0, LANES)], meta_s.at[pl.ds(k * LANES, LANES)], msem)
            cp.start()
            cp.wait()

        def zcopy(first_row, n_rows):
            return pltpu.make_async_copy(
                zsrc.at[pl.ds(0, n_rows * SUBLANES)],
                xs_hbm.at[pl.ds(pl.multiple_of(first_row * SUBLANES, SUBLANES), n_rows * SUBLANES)], zsem)

        for c in range(N_CLASSES):
            tile = meta_s[c]
            fill_c = meta_s[LANES + c]
            rem = TILE_ROWS - fill_c
            for bit in range(TILE_ROWS_LOG2, -1, -1):
                size = 1 << bit
                first = (tile << TILE_ROWS_LOG2) + fill_c + (rem & ~(2 * size - 1))

                @pl.when((rem & size) != 0)
                def _(first=first, size=size):
                    cp = zcopy(first, size)
                    cp.start()
                    cp.wait()

        def tail(k, carry):
            cp = zcopy(k << TILE_ROWS_LOG2, TILE_ROWS)
            cp.start()
            cp.wait()
            return carry

        lax.fori_loop(meta_s[2 * LANES], n_tiles, tail, 0)
        sent(slot).wait()


def _ssm_tables(a_re, a_im, log_step, b_re, b_im, c_re, c_im, seg):
    g, p = a_re.shape
    h = b_re.shape[-1]
    n_slab = g // SLAB_GROUPS
    lr, li = a_re.astype(F32), a_im.astype(F32)
    step = jnp.exp(log_step.astype(F32))[:, None]
    mag = jnp.exp(lr * step)
    ab_re = mag * jnp.cos(li * step)
    ab_im = mag * jnp.sin(li * step)
    den = lr * lr + li * li
    nr, ni = ab_re - 1.0, ab_im
    q_re = (nr * lr + ni * li) / den
    q_im = (ni * lr - nr * li) / den
    br, bi = b_re.astype(F32), b_im.astype(F32)
    bb_re = q_re[..., None] * br - q_im[..., None] * bi
    bb_im = q_re[..., None] * bi + q_im[..., None] * br

    eye = jnp.eye(SLAB_GROUPS, dtype=F32)

    def b_slab(bb):
        bb = bb.reshape(n_slab, SLAB_GROUPS, p, h)
        w = jnp.einsum('mgph,gk->mghkp', bb, eye)
        return w.reshape(n_slab, SLAB_GROUPS * h, SLAB_GROUPS * p)

    wb = jnp.concatenate([b_slab(bb_re), b_slab(bb_im)], axis=-1).astype(BF16)

    def c_slab(cc):
        cc = cc.astype(F32).reshape(n_slab, SLAB_GROUPS, h, p)
        w = jnp.einsum('mghp,gk->mgpkh', cc, eye)
        return w.reshape(n_slab, SLAB_GROUPS * p, SLAB_GROUPS * h)

    wc_re = c_slab(c_re).astype(BF16)
    wc_im = c_slab(-c_im).astype(BF16)

    def apow(nn):
        nn = nn.astype(F32)[:, None, None]
        m_ = jnp.exp(nn * (lr * step))
        ang = nn * (li * step)
        return (m_ * jnp.cos(ang)).reshape(-1, g * p), (m_ * jnp.sin(ang)).reshape(-1, g * p)

    a1r, a1i = apow(jnp.ones((1,), F32))
    a_r = jnp.broadcast_to(a1r, (SUBLANES, g * p))
    a_i = jnp.broadcast_to(a1i, (SUBLANES, g * p))
    pr, pi = apow(jnp.arange(1, seg + 1))
    ap_r = jnp.repeat(pr, SUBLANES, axis=0)
    ap_i = jnp.repeat(pi, SUBLANES, axis=0)
    ks = jnp.array([1, 2, 4])
    mr, mi = apow(ks * seg)
    rows = jnp.arange(SUBLANES)[None, :, None]
    mask = (rows >= ks[:, None, None]).astype(F32)
    mk_r = mr[:, None, :] * mask
    mk_i = mi[:, None, :] * mask
    sr, si = apow(jnp.full((1,), seg, F32))
    as_r = jnp.broadcast_to(sr, (SUBLANES, g * p))
    as_i = jnp.broadcast_to(si, (SUBLANES, g * p))
    return wb, a_r, a_i, ap_r, ap_i, mk_r, mk_i, as_r, as_i, wc_re, wc_im


def _router_weights(w_coarse, b_coarse, w_fine, b_fine):
    d = w_coarse.shape[0]
    pad = ROUTER_LANES - N_EXPERT_GROUPS - N_EXPERTS
    wr = jnp.concatenate([w_coarse, w_fine, jnp.zeros((d, pad), F32)], axis=-1).astype(BF16)
    br = jnp.concatenate([b_coarse, b_fine, jnp.zeros((pad,), F32)]).reshape(1, -1).astype(F32)
    return wr, br


def _mixer(x, norm_mix, w_in, pool_w, pool_scale, a_re, a_im, log_step, b_re, b_im, c_re, c_im,
           d_skip, glu_w, glu_b, w_out, norm_ffn, wr, br):
    b, l, d = x.shape
    d_pool = pool_scale.shape[-1]
    d_ssm = d_skip.shape[-1]
    tc = TIME_TILE
    assert l % tc == 0 and d_pool == len(POOL_WINDOWS) * POOL_GROUP and d_ssm % LANES == 0
    assert d == SUBLANES * LANES, "one token row must fill exactly one (8, 128) tile"
    assert tc == TILE_ROWS, "a mixer tile must never need more than one new sorted tile per class"
    assert tc <= 3 * LANES and N_CLASSES <= LANES
    seg = tc // SUBLANES
    n_state = a_re.shape[0] * a_re.shape[1]
    tables = _ssm_tables(a_re, a_im, log_step, b_re, b_im, c_re, c_im, seg)
    wb, a_r, a_i, ap_r, ap_i, mk_r, mk_i, as_r, as_i, wc_re, wc_im = tables

    i = jnp.arange(tc)
    src = (i % SUBLANES) * seg + i // SUBLANES
    perm = (src[:, None] == i[None, :]).astype(BF16)
    permt = perm.T
    tri = (i[None, :] < i[:, None]).astype(BF16)

    row2 = lambda a: a.reshape(1, -1).astype(F32)
    operands = [
        x, row2(norm_mix), w_in.astype(BF16), pool_w.astype(BF16), row2(pool_scale), perm, permt, wb,
        a_r, a_i, ap_r, ap_i, mk_r, mk_i, as_r, as_i, wc_re, wc_im,
        row2(d_skip), glu_w.astype(BF16), row2(glu_b), w_out.astype(BF16),
        row2(norm_ffn), wr, br, tri,
    ]

    def whole(a):
        nd = a.ndim
        return pl.BlockSpec(a.shape, lambda bi, ni, nd=nd: (0,) * nd)

    n_t = l // tc
    n_tiles = (b * l) // TILE_ROWS + N_CLASSES
    in_specs = [pl.BlockSpec((1, tc, d), lambda bi, ni: (bi, ni, 0))] + [whole(a) for a in operands[1:]]
    kern = functools.partial(_mixer_kernel, tc=tc, d_pool=d_pool, d_ssm=d_ssm, n_tiles=n_tiles)
    return pl.pallas_call(
        kern,
        grid=(b, n_t),
        in_specs=in_specs,
        out_specs=[
            pl.BlockSpec(memory_space=pl.ANY),
            pl.BlockSpec((1, 1, tc), lambda bi, ni: (bi * n_t + ni, 0, 0)),
            pl.BlockSpec((1, 1, LANES), lambda bi, ni: (bi * n_t + ni, 0, 0)),
            pl.BlockSpec((SUBLANES, LANES), lambda bi, ni: (0, 0)),
        ],
        out_shape=[
            jax.ShapeDtypeStruct((n_tiles * TILE_ROWS * SUBLANES, LANES), F32),
            jax.ShapeDtypeStruct((b * n_t, 1, tc), I32),
            jax.ShapeDtypeStruct((b * n_t, 1, LANES), I32),
            jax.ShapeDtypeStruct((SUBLANES, LANES), I32),
        ],
        scratch_shapes=[
            pltpu.VMEM((POOL_HIST + tc, d_pool), F32),
            pltpu.VMEM((POOL_HIST + tc, d_pool), F32),
            pltpu.VMEM((POOL_HIST + tc, d_pool - POOL_GROUP), F32),
            pltpu.VMEM((POOL_HIST + tc, d_pool - 2 * POOL_GROUP), F32),
            pltpu.VMEM((tc, n_state), F32),
            pltpu.VMEM((tc, n_state), F32),
            pltpu.VMEM((SUBLANES, n_state), F32),
            pltpu.VMEM((SUBLANES, n_state), F32),
            pltpu.VMEM((2, tc * SUBLANES, LANES), F32),
            pltpu.VMEM((SUBLANES, tc), I32),
            pltpu.SMEM((3 * LANES,), I32),
            pltpu.VMEM((SUBLANES, LANES), F32),
            pltpu.SemaphoreType.DMA((2,)),
            pltpu.SemaphoreType.DMA,
            pltpu.SemaphoreType.DMA,
        ],
        compiler_params=pltpu.CompilerParams(
            dimension_semantics=("arbitrary", "arbitrary"), vmem_limit_bytes=VMEM_LIMIT),
        name="mixer",
    )(*operands)


def _experts_kernel(alloc_ref, fin_ref, x_ref, nf_ref, wr_ref, br_ref, wg_hbm, wu_hbm, wd_hbm,
                    nfin_ref, o_ref, wg_ref, wu_ref, wd_ref, tcls, tnv, wsem, *, n_steps):
    i = pl.program_id(0)
    n_tiles = pl.num_programs(0)
    rows = TILE_ROWS

    @pl.when(i == 0)
    def _():
        copies = [pltpu.make_async_copy(src, dst, wsem.at[j])
                  for j, (src, dst) in enumerate(((wg_hbm, wg_ref), (wu_hbm, wu_ref), (wd_hbm, wd_ref)))]
        for cp in copies:
            cp.start()
        free = fin_ref[2 * LANES]

        def init(k, carry):
            tcls[k] = jnp.minimum(k, N_CLASSES - 1)
            tnv[k] = jnp.where(k < free, rows, 0)
            return carry

        lax.fori_loop(0, n_tiles, init, 0)

        def log(s, carry):
            for c in range(N_CLASSES):
                t = alloc_ref[s * LANES + c]
                tcls[jnp.where(t >= 0, t, n_tiles)] = c
            return carry

        lax.fori_loop(0, n_steps, log, 0)
        for c in range(N_CLASSES):
            tnv[fin_ref[c]] = fin_ref[LANES + c]
        for cp in copies:
            cp.wait()

    @pl.when(tnv[i] == 0)
    def _():
        o_ref[...] = jnp.zeros_like(o_ref)

    @pl.when(tnv[i] > 0)
    def _():
        c = tcls[i]
        g = (c >= PAIRS_PER_GROUP).astype(I32) + (c >= 2 * PAIRS_PER_GROUP).astype(I32) \
            + (c >= 3 * PAIRS_PER_GROUP).astype(I32)
        p = c - g * PAIRS_PER_GROUP
        a = (p >= 3).astype(I32) + (p >= 5).astype(I32)
        b = p - ((a * (7 - a)) >> 1) + a + 1
        ea = g * EXPERTS_PER_GROUP + a
        eb = g * EXPERTS_PER_GROUP + b
        x = jnp.concatenate([x_ref[pl.ds(k, rows, stride=SUBLANES), :] for k in range(SUBLANES)], axis=-1)
        hb = _rms(x, nf_ref[...]).astype(BF16)
        logits = _dot(hb, wr_ref[...]) + br_ref[...]
        lane = lax.broadcasted_iota(I32, (rows, ROUTER_LANES), 1)
        is_c = lane < N_EXPERT_GROUPS
        cmax = jnp.max(jnp.where(is_c, logits, NEG), axis=-1, keepdims=True)
        psum = jnp.sum(jnp.where(is_c, jnp.exp(logits - cmax), 0.0), axis=-1, keepdims=True)
        cg = jnp.sum(jnp.where(lane == g, logits, 0.0), axis=-1, keepdims=True)
        p_g = jnp.exp(cg - cmax) / psum
        va = jnp.sum(jnp.where(lane == N_EXPERT_GROUPS + ea, logits, 0.0), axis=-1, keepdims=True)
        vb = jnp.sum(jnp.where(lane == N_EXPERT_GROUPS + eb, logits, 0.0), axis=-1, keepdims=True)
        vmax = jnp.maximum(va, vb)
        exa = jnp.exp(va - vmax)
        exb = jnp.exp(vb - vmax)
        den = exa + exb
        y = jnp.zeros_like(x)
        for e, w in ((ea, p_g * (exa / den)), (eb, p_g * (exb / den))):
            act = jax.nn.silu(_dot(hb, wg_ref[e])) * _dot(hb, wu_ref[e])
            y = y + _dot((act * w).astype(BF16), wd_ref[e])
        out = _rms(x + y, nfin_ref[...])
        for k in range(SUBLANES):
            o_ref[pl.ds(k, rows, stride=SUBLANES), :] = out[:, k * LANES:(k + 1) * LANES]


def _experts(xs, alloc, fin, norm_ffn, wr, br, w_gate, w_up, w_down, norm_final):
    n_e, d, f = w_gate.shape
    assert n_e == N_EXPERTS
    blk = TILE_ROWS * SUBLANES
    n_tiles = xs.shape[0] // blk
    n_steps = alloc.shape[0] // LANES
    row2 = lambda a: a.reshape(1, -1).astype(F32)
    const = lambda shape: pl.BlockSpec(shape, lambda i, al, fi: (0,) * len(shape))
    anyspec = pl.BlockSpec(memory_space=pl.ANY)
    grid_spec = pltpu.PrefetchScalarGridSpec(
        num_scalar_prefetch=2,
        grid=(n_tiles,),
        in_specs=[
            pl.BlockSpec((blk, LANES), lambda i, al, fi: (i, 0)),
            const((1, d)), const((d, ROUTER_LANES)), const((1, ROUTER_LANES)),
            anyspec, anyspec, anyspec,
            const((1, d)),
        ],
        out_specs=pl.BlockSpec((blk, LANES), lambda i, al, fi: (i, 0)),
        scratch_shapes=[
            pltpu.VMEM((n_e, d, f), BF16),
            pltpu.VMEM((n_e, d, f), BF16),
            pltpu.VMEM((n_e, f, d), BF16),
            pltpu.SMEM((n_tiles + 1,), I32),
            pltpu.SMEM((n_tiles + 1,), I32),
            pltpu.SemaphoreType.DMA((3,)),
        ],
    )
    return pl.pallas_call(
        functools.partial(_experts_kernel, n_steps=n_steps),
        grid_spec=grid_spec,
        out_shape=jax.ShapeDtypeStruct(xs.shape, F32),
        compiler_params=pltpu.CompilerParams(
            dimension_semantics=("arbitrary",), vmem_limit_bytes=VMEM_LIMIT),
        name="experts",
    )(alloc, fin, xs, row2(norm_ffn), wr, br,
      w_gate.astype(BF16), w_up.astype(BF16), w_down.astype(BF16), row2(norm_final))


def _unsort_kernel(dst_hbm, ys_hbm, o_ref, idx, buf, isem, gsem):
    i = pl.program_id(0)
    nt = pl.num_programs(0)
    slot = lax.rem(i, 2)
    rows = o_ref.shape[0]

    def idx_copy(tile, s3):
        return pltpu.make_async_copy(dst_hbm.at[tile], idx.at[pl.ds(s3 * rows, rows)], isem.at[s3])

    def gather_start(s3, bslot):
        def body(r2, c):
            for prio in range(2):
                r = r2 * 2 + prio
                src = idx[s3 * rows + r]
                pltpu.make_async_copy(ys_hbm.at[pl.ds(pl.multiple_of(src * SUBLANES, SUBLANES), SUBLANES)],
                                      buf.at[bslot, pl.ds(pl.multiple_of(r * SUBLANES, SUBLANES), SUBLANES)],
                                      gsem.at[bslot]).start(priority=prio)
            return c

        lax.fori_loop(0, rows // 2, body, 0, unroll=4)

    @pl.when(i == 0)
    def _():
        idx_copy(0, 0).start()
        idx_copy(0, 0).wait()

        @pl.when(nt > 1)
        def _():
            idx_copy(1, 1).start()
            idx_copy(1, 1).wait()

        gather_start(0, 0)

    @pl.when(i + 2 < nt)
    def _():
        idx_copy(i + 2, lax.rem(i + 2, 3)).start()

    @pl.when((i >= 1) & (i + 1 < nt))
    def _():
        idx_copy(i + 1, lax.rem(i + 1, 3)).wait()

    @pl.when(i + 1 < nt)
    def _():
        gather_start(lax.rem(i + 1, 3), 1 - slot)

    pltpu.make_async_copy(ys_hbm.at[pl.ds(0, rows * SUBLANES)], buf.at[slot], gsem.at[slot]).wait()
    o_ref[...] = jnp.concatenate(
        [buf[slot, pl.ds(c, rows, stride=SUBLANES), :] for c in range(SUBLANES)], axis=-1)


def _unsort(ys, dst, n_tok, d):
    n_tiles, _, rows = dst.shape
    anyspec = pl.BlockSpec(memory_space=pl.ANY)
    return pl.pallas_call(
        _unsort_kernel,
        grid=(n_tiles,),
        in_specs=[anyspec, anyspec],
        out_specs=pl.BlockSpec((rows, d), lambda i: (i, 0)),
        out_shape=jax.ShapeDtypeStruct((n_tok, d), F32),
        scratch_shapes=[
            pltpu.SMEM((3 * rows,), I32),
            pltpu.VMEM((2, rows * SUBLANES, LANES), F32),
            pltpu.SemaphoreType.DMA((3,)),
            pltpu.SemaphoreType.DMA((2,)),
        ],
        compiler_params=pltpu.CompilerParams(
            dimension_semantics=("arbitrary",), vmem_limit_bytes=VMEM_LIMIT),
        name="unsort",
    )(dst.reshape(n_tiles, rows), ys)


def kernel(x, norm_mix, w_in, pool_w, pool_scale, ssm_a_re, ssm_a_im, ssm_log_step, ssm_b_re, ssm_b_im, ssm_c_re, ssm_c_im, ssm_d, glu_w, glu_b, w_out, norm_ffn, router_coarse_w, router_coarse_b, router_fine_w, router_fine_b, exp_w_gate, exp_w_up, exp_w_down, norm_final):
    assert norm_mix.shape[0] == 1, "the experts call fuses the final norm: single-layer blocks only"
    b, l, d = x.shape
    wr, br = _router_weights(router_coarse_w[0], router_coarse_b[0], router_fine_w[0], router_fine_b[0])
    xs, dst, alloc, fin = _mixer(
        x, norm_mix[0], w_in[0], pool_w[0], pool_scale[0], ssm_a_re[0], ssm_a_im[0], ssm_log_step[0],
        ssm_b_re[0], ssm_b_im[0], ssm_c_re[0], ssm_c_im[0], ssm_d[0], glu_w[0], glu_b[0], w_out[0],
        norm_ffn[0], wr, br)
    ys = _experts(xs, alloc.reshape(-1), fin.reshape(-1), norm_ffn[0], wr, br,
                  exp_w_gate[0], exp_w_up[0], exp_w_down[0], norm_final)
    y = _unsort(ys, dst, b * l, d)
    return y.reshape(b, l, d)
```

```python
import functools

import jax
import jax.numpy as jnp
from jax import lax
from jax.experimental import pallas as pl
from jax.experimental.pallas import tpu as pltpu

EPS = 1e-6
POOL_WINDOWS = (2, 4, 8, 16)
POOL_GROUP = 128
POOL_TAIL = 16
POOL_HIST = 32
SSM_GROUP = 16
SSM_STATE = 64
N_EXPERT_GROUPS = 4
EXPERTS_PER_GROUP = 4
N_EXPERTS = 16
PAIRS_PER_GROUP = 6
N_CLASSES = N_EXPERT_GROUPS * PAIRS_PER_GROUP

SUBLANES = 8
LANES = 128
SLAB_GROUPS = 8
STATE_SLAB = SLAB_GROUPS * SSM_STATE
SCAN_LANES = 1024
TIME_TILE = 256
TILE_ROWS = 256
TILE_ROWS_LOG2 = 8
ROUTER_LANES = 128
VMEM_LIMIT = 56 * 1024 * 1024
NEG = -3.0e38

F32 = jnp.float32
BF16 = jnp.bfloat16
I32 = jnp.int32
NT_DIMS = (((1,), (1,)), ((), ()))


def _dot(a, b):
    return jnp.dot(a, b, preferred_element_type=F32)


def _dot_nt(a, b):
    return lax.dot_general(a, b, NT_DIMS, preferred_element_type=F32)


def _rms(x, g):
    return x * lax.rsqrt(jnp.mean(x * x, axis=-1, keepdims=True) + EPS) * g


def _rows8(*rows):
    r8 = lax.broadcasted_iota(I32, (SUBLANES, LANES), 0)
    out = jnp.zeros((SUBLANES, LANES), F32)
    for k, row in enumerate(rows):
        out = jnp.where(r8 == k, jnp.broadcast_to(row, (SUBLANES, LANES)), out)
    return out


def _route(logits, lane):
    is_c = lane < N_EXPERT_GROUPS
    cm = jnp.where(is_c, logits, NEG)
    cmax = jnp.max(cm, axis=-1, keepdims=True)
    gi = jnp.min(jnp.where(cm == cmax, lane, ROUTER_LANES), axis=-1, keepdims=True)
    lo = N_EXPERT_GROUPS + gi * EXPERTS_PER_GROUP
    insel = (lane >= lo) & (lane < lo + EXPERTS_PER_GROUP)
    fs = jnp.where(insel, logits, NEG)
    v0 = jnp.max(fs, axis=-1, keepdims=True)
    i0 = jnp.min(jnp.where(insel & (fs == v0), lane, ROUTER_LANES), axis=-1, keepdims=True)
    rest = insel & (lane != i0)
    fs2 = jnp.where(rest, logits, NEG)
    v1 = jnp.max(fs2, axis=-1, keepdims=True)
    i1 = jnp.min(jnp.where(rest & (fs2 == v1), lane, ROUTER_LANES), axis=-1, keepdims=True)
    return gi, lo, i0, i1


def _mixer_kernel(x_ref, nm_ref, win_ref, pw_ref, ps_ref, perm_ref, permt_ref, wb_ref,
                  are_ref, aim_ref, apre_ref, apim_ref, mkre_ref, mkim_ref, asre_ref, asim_ref,
                  wcre_ref, wcim_ref, d_ref, gw_ref, gb_ref, wout_ref, nf_ref, wr_ref, br_ref, tri_ref,
                  xs_hbm, dst_ref, alloc_ref, fin_ref,
                  zpad, lv2, lv4, lv8, bur, bui, st_r, st_i, x1t, meta_v, meta_s, alloc_v, ssem, msem, zsem,
                  *, tc, d_pool, d_ssm, n_tiles):
    bq = pl.program_id(0)
    n = pl.program_id(1)
    n_t = pl.num_programs(1)
    step = bq * n_t + n
    n_steps = pl.num_programs(0) * n_t
    slot = lax.rem(step, 2)
    seg = tc // SUBLANES
    n_slab = d_ssm // LANES
    n_chunk = n_slab * STATE_SLAB // SCAN_LANES
    per_iter = tc // (2 * n_chunk * seg)
    assert per_iter * 2 * n_chunk * seg == tc
    lane1 = lax.broadcasted_iota(I32, (1, LANES), 1)

    @pl.when(n == 0)
    def _():
        zpad[0:POOL_HIST, :] = jnp.zeros((POOL_HIST, d_pool), F32)
        st_r[...] = jnp.zeros_like(st_r)
        st_i[...] = jnp.zeros_like(st_i)

    @pl.when(step == 0)
    def _():
        lane_f = lane1.astype(F32)
        alloc_v[...] = _rows8(jnp.where(lane1 < N_CLASSES, lane_f, 0.0), jnp.zeros((1, LANES), F32),
                              jnp.full((1, LANES), float(N_CLASSES), F32))

    meta_cp = pltpu.make_async_copy(meta_v.at[0], meta_s.at[pl.ds(0, tc)], msem)

    def sent(s_):
        return pltpu.make_async_copy(x1t.at[s_], xs_hbm.at[pl.ds(0, tc * SUBLANES)], ssem.at[s_])

    def send_row(s_, r, prio):
        d = meta_s[r]
        pltpu.make_async_copy(x1t.at[s_, pl.ds(pl.multiple_of(r * SUBLANES, SUBLANES), SUBLANES)],
                              xs_hbm.at[pl.ds(pl.multiple_of(d * SUBLANES, SUBLANES), SUBLANES)],
                              ssem.at[s_]).start(priority=prio)

    def send_rows(s_):
        def send(r2, carry):
            for prio in range(2):
                send_row(s_, r2 * 2 + prio, prio)
            return carry

        lax.fori_loop(0, tc // 2, send, 0, unroll=4)

    def send_some(r_first):
        for k in range(per_iter):
            send_row(1 - slot, r_first + k, k % 2)

    x = x_ref[0]
    hn = _rms(x, nm_ref[...]).astype(BF16)
    z = _dot(hn, win_ref[...])

    zp = z[:, :d_pool]
    top = POOL_HIST + tc
    zpad[POOL_HIST:top, :] = zp
    g1, g2, g3 = POOL_GROUP, 2 * POOL_GROUP, 3 * POOL_GROUP
    lv2[8:top, :] = zpad[8:top, :] + zpad[7:top - 1, :]
    lv4[16:top, :] = lv2[16:top, g1:] + lv2[14:top - 2, g1:]
    lv8[24:top, :] = lv4[24:top, g1:] + lv4[20:top - 4, g1:]
    wins = (lv2[POOL_HIST:top, 0:g1], lv4[POOL_HIST:top, 0:g1], lv8[POOL_HIST:top, 0:g1],
            lv8[POOL_HIST:top, g1:] + lv8[POOL_HIST - 8:top - 8, g1:])
    t = n * tc + lax.broadcasted_iota(I32, (tc, 1), 0)
    pooled = []
    for gi, w in enumerate(POOL_WINDOWS):
        cols = slice(gi * POOL_GROUP, (gi + 1) * POOL_GROUP)
        cnt = jnp.minimum(t + 1, w).astype(F32)
        p = wins[gi] / cnt - zp[:, cols]
        pooled.append(_dot(p.astype(BF16), pw_ref[gi]))
    y_pool = jnp.concatenate(pooled, axis=-1) * ps_ref[...]
    zpad[POOL_HIST - POOL_TAIL:POOL_HIST, :] = zpad[top - POOL_TAIL:top, :]

    u = z[:, d_pool:]
    up = _dot(perm_ref[...], u.astype(BF16))
    upb = up.astype(BF16)
    for m in range(n_slab):
        r = _dot(upb[:, m * LANES:(m + 1) * LANES], wb_ref[m])
        bur[:, m * STATE_SLAB:(m + 1) * STATE_SLAB] = r[:, :STATE_SLAB]
        bui[:, m * STATE_SLAB:(m + 1) * STATE_SLAB] = r[:, STATE_SLAB:]

    def scan_all(sending):
        row = lax.broadcasted_iota(I32, (SUBLANES, SCAN_LANES), 0)
        for c in range(n_chunk):
            scan_chunk(c, row, sending)

    def scan_chunk(c, row, sending):
        cs = slice(c * SCAN_LANES, (c + 1) * SCAN_LANES)
        ar = are_ref[:, cs]
        ai = aim_ref[:, cs]

        def scan_body(s, carry):
            xr, xi = carry
            r0 = pl.multiple_of(s * SUBLANES, SUBLANES)
            nxr = ar * xr - ai * xi + bur[pl.ds(r0, SUBLANES), cs]
            nxi = ar * xi + ai * xr + bui[pl.ds(r0, SUBLANES), cs]
            bur[pl.ds(r0, SUBLANES), cs] = nxr
            bui[pl.ds(r0, SUBLANES), cs] = nxi
            if sending:
                send_some(((2 * c) * seg + s) * per_iter)
            return nxr, nxi

        zero = jnp.zeros((SUBLANES, SCAN_LANES), F32)
        er, ei = lax.fori_loop(0, seg, scan_body, (zero, zero), unroll=4)

        fr = jnp.where(row == 0, st_r[:, cs], pltpu.roll(er, 1, 0))
        fi = jnp.where(row == 0, st_i[:, cs], pltpu.roll(ei, 1, 0))
        for ki, k in enumerate((1, 2, 4)):
            mr = mkre_ref[ki, :, cs]
            mi = mkim_ref[ki, :, cs]
            rr = pltpu.roll(fr, k, 0)
            ri = pltpu.roll(fi, k, 0)
            fr, fi = fr + mr * rr - mi * ri, fi + mr * ri + mi * rr
        asr = asre_ref[:, cs]
        asi = asim_ref[:, cs]
        nsr = asr * fr - asi * fi + er
        nsi = asr * fi + asi * fr + ei
        st_r[:, cs] = pltpu.roll(nsr, 1, 0)
        st_i[:, cs] = pltpu.roll(nsi, 1, 0)

        def fix_body(s, carry):
            r0 = pl.multiple_of(s * SUBLANES, SUBLANES)
            pr = apre_ref[pl.ds(r0, SUBLANES), cs]
            pi = apim_ref[pl.ds(r0, SUBLANES), cs]
            bur[pl.ds(r0, SUBLANES), cs] = bur[pl.ds(r0, SUBLANES), cs] + pr * fr - pi * fi
            bui[pl.ds(r0, SUBLANES), cs] = bui[pl.ds(r0, SUBLANES), cs] + pr * fi + pi * fr
            if sending:
                send_some(((2 * c + 1) * seg + s) * per_iter)
            return carry

        lax.fori_loop(0, seg, fix_body, 0, unroll=4)

    @pl.when(step == 0)
    def _():
        scan_all(False)

    @pl.when(step >= 1)
    def _():
        meta_cp.wait()
        scan_all(True)

    ys = []
    for m in range(n_slab):
        cs = slice(m * STATE_SLAB, (m + 1) * STATE_SLAB)
        ys.append(_dot(bur[:, cs].astype(BF16), wcre_ref[m]) + _dot(bui[:, cs].astype(BF16), wcim_ref[m]))
    y = jnp.concatenate(ys, axis=-1) + d_ref[...] * up
    y = jax.nn.gelu(y)
    y = y * jax.nn.sigmoid(_dot(y.astype(BF16), gw_ref[...]) + gb_ref[...])
    y_ssm = _dot(permt_ref[...], y.astype(BF16))

    mix = _dot(y_pool.astype(BF16), wout_ref[0:d_pool, :]) + _dot(y_ssm.astype(BF16), wout_ref[d_pool:, :])
    x1 = x + mix

    hb = _rms(x1, nf_ref[...]).astype(BF16)
    logits = _dot(hb, wr_ref[...]) + br_ref[...]
    lane = lax.broadcasted_iota(I32, (tc, ROUTER_LANES), 1)
    gi, lo, i0, i1 = _route(logits, lane)
    ea = jnp.minimum(i0, i1) - lo
    eb = jnp.maximum(i0, i1) - lo
    cls = gi * PAIRS_PER_GROUP + ((ea * (7 - ea)) >> 1) + (eb - ea - 1)
    oh = (lane == cls).astype(F32)
    ohb = oh.astype(BF16)
    before = _dot(tri_ref[...], ohb)
    ones8 = jnp.ones((SUBLANES, ROUTER_LANES), BF16)
    rank = _dot_nt(ones8, (before * oh).astype(BF16))[0:1, :]
    cnt = jnp.sum(oh, axis=0, keepdims=True)

    st = alloc_v[...]
    open_t, fill, free = st[0:1], st[1:2], st[2:3]
    room = float(TILE_ROWS) - fill
    spill = cnt > room
    spill_f = spill.astype(F32)
    upper = (lax.broadcasted_iota(I32, (LANES, LANES), 0) < lax.broadcasted_iota(I32, (LANES, LANES), 1))
    earlier = _dot(_rows8(spill_f).astype(BF16), upper.astype(F32).astype(BF16))[0:1]
    new_t = free + earlier
    per_row = _dot_nt(_rows8(open_t, new_t, fill).astype(BF16), ohb)
    open_r, new_r, fill_r = per_row[0:1], per_row[1:2], per_row[2:3]
    room_r = float(TILE_ROWS) - fill_r
    dst = jnp.where(rank < room_r, open_r * float(TILE_ROWS) + fill_r + rank,
                    new_r * float(TILE_ROWS) + rank - room_r).astype(I32)
    dst_ref[0] = dst
    alloc_ref[0] = jnp.where(spill, new_t, -1.0).astype(I32)
    new_state = _rows8(jnp.where(spill, new_t, open_t), jnp.where(spill, cnt - room, fill + cnt),
                       free + jnp.sum(spill_f, axis=-1, keepdims=True))
    alloc_v[...] = new_state
    fin_ref[...] = new_state.astype(I32)

    @pl.when(step >= 2)
    def _():
        sent(slot).wait()

    for c in range(SUBLANES):
        x1t[slot, pl.ds(c, tc, stride=SUBLANES), :] = x1[:, c * LANES:(c + 1) * LANES]
    meta_v[0:1, :] = dst
    meta_cp.start()

    @pl.when(step == n_steps - 1)
    def _():
        meta_cp.wait()
        send_rows(slot)

        @pl.when(step >= 1)
        def _():
            sent(1 - slot).wait()


        zsrc = x1t.at[1 - slot]
        zsrc[...] = jnp.zeros((tc * SUBLANES, LANES), F32)
        meta_v[:, 0:LANES] = new_state.astype(I32)
        for k in range(3):
            cp = pltpu.make_async_copy(meta_v.at[k, pl.ds(0, LANES)], meta_s.at[pl.ds(k * LANES, LANES)], msem)
            cp.start()
            cp.wait()

        def zcopy(first_row, n_rows):
            return pltpu.make_async_copy(
                zsrc.at[pl.ds(0, n_rows * SUBLANES)],
                xs_hbm.at[pl.ds(pl.multiple_of(first_row * SUBLANES, SUBLANES), n_rows * SUBLANES)], zsem)

        for c in range(N_CLASSES):
            tile = meta_s[c]
            fill_c = meta_s[LANES + c]
            rem = TILE_ROWS - fill_c
            for bit in range(TILE_ROWS_LOG2, -1, -1):
                size = 1 << bit
                first = (tile << TILE_ROWS_LOG2) + fill_c + (rem & ~(2 * size - 1))

                @pl.when((rem & size) != 0)
                def _(first=first, size=size):
                    cp = zcopy(first, size)
                    cp.start()
                    cp.wait()

        def tail(k, carry):
            cp = zcopy(k << TILE_ROWS_LOG2, TILE_ROWS)
            cp.start()
            cp.wait()
            return carry

        lax.fori_loop(meta_s[2 * LANES], n_tiles, tail, 0)
        sent(slot).wait()


def _ssm_tables(a_re, a_im, log_step, b_re, b_im, c_re, c_im, seg):
    g, p = a_re.shape
    h = b_re.shape[-1]
    n_slab = g // SLAB_GROUPS
    lr, li = a_re.astype(F32), a_im.astype(F32)
    step = jnp.exp(log_step.astype(F32))[:, None]
    mag = jnp.exp(lr * step)
    ab_re = mag * jnp.cos(li * step)
    ab_im = mag * jnp.sin(li * step)
    den = lr * lr + li * li
    nr, ni = ab_re - 1.0, ab_im
    q_re = (nr * lr + ni * li) / den
    q_im = (ni * lr - nr * li) / den
    br, bi = b_re.astype(F32), b_im.astype(F32)
    bb_re = q_re[..., None] * br - q_im[..., None] * bi
    bb_im = q_re[..., None] * bi + q_im[..., None] * br

    eye = jnp.eye(SLAB_GROUPS, dtype=F32)

    def b_slab(bb):
        bb = bb.reshape(n_slab, SLAB_GROUPS, p, h)
        w = jnp.einsum('mgph,gk->mghkp', bb, eye)
        return w.reshape(n_slab, SLAB_GROUPS * h, SLAB_GROUPS * p)

    wb = jnp.concatenate([b_slab(bb_re), b_slab(bb_im)], axis=-1).astype(BF16)

    def c_slab(cc):
        cc = cc.astype(F32).reshape(n_slab, SLAB_GROUPS, h, p)
        w = jnp.einsum('mghp,gk->mgpkh', cc, eye)
        return w.reshape(n_slab, SLAB_GROUPS * p, SLAB_GROUPS * h)

    wc_re = c_slab(c_re).astype(BF16)
    wc_im = c_slab(-c_im).astype(BF16)

    def apow(nn):
        nn = nn.astype(F32)[:, None, None]
        m_ = jnp.exp(nn * (lr * step))
        ang = nn * (li * step)
        return (m_ * jnp.cos(ang)).reshape(-1, g * p), (m_ * jnp.sin(ang)).reshape(-1, g * p)

    a1r, a1i = apow(jnp.ones((1,), F32))
    a_r = jnp.broadcast_to(a1r, (SUBLANES, g * p))
    a_i = jnp.broadcast_to(a1i, (SUBLANES, g * p))
    pr, pi = apow(jnp.arange(1, seg + 1))
    ap_r = jnp.repeat(pr, SUBLANES, axis=0)
    ap_i = jnp.repeat(pi, SUBLANES, axis=0)
    ks = jnp.array([1, 2, 4])
    mr, mi = apow(ks * seg)
    rows = jnp.arange(SUBLANES)[None, :, None]
    mask = (rows >= ks[:, None, None]).astype(F32)
    mk_r = mr[:, None, :] * mask
    mk_i = mi[:, None, :] * mask
    sr, si = apow(jnp.full((1,), seg, F32))
    as_r = jnp.broadcast_to(sr, (SUBLANES, g * p))
    as_i = jnp.broadcast_to(si, (SUBLANES, g * p))
    return wb, a_r, a_i, ap_r, ap_i, mk_r, mk_i, as_r, as_i, wc_re, wc_im


def _router_weights(w_coarse, b_coarse, w_fine, b_fine):
    d = w_coarse.shape[0]
    pad = ROUTER_LANES - N_EXPERT_GROUPS - N_EXPERTS
    wr = jnp.concatenate([w_coarse, w_fine, jnp.zeros((d, pad), F32)], axis=-1).astype(BF16)
    br = jnp.concatenate([b_coarse, b_fine, jnp.zeros((pad,), F32)]).reshape(1, -1).astype(F32)
    return wr, br


def _mixer(x, norm_mix, w_in, pool_w, pool_scale, a_re, a_im, log_step, b_re, b_im, c_re, c_im,
           d_skip, glu_w, glu_b, w_out, norm_ffn, wr, br):
    b, l, d = x.shape
    d_pool = pool_scale.shape[-1]
    d_ssm = d_skip.shape[-1]
    tc = TIME_TILE
    assert l % tc == 0 and d_pool == len(POOL_WINDOWS) * POOL_GROUP and d_ssm % LANES == 0
    assert d == SUBLANES * LANES, "one token row must fill exactly one (8, 128) tile"
    assert tc == TILE_ROWS, "a mixer tile must never need more than one new sorted tile per class"
    assert tc <= 3 * LANES and N_CLASSES <= LANES
    seg = tc // SUBLANES
    n_state = a_re.shape[0] * a_re.shape[1]
    tables = _ssm_tables(a_re, a_im, log_step, b_re, b_im, c_re, c_im, seg)
    wb, a_r, a_i, ap_r, ap_i, mk_r, mk_i, as_r, as_i, wc_re, wc_im = tables

    i = jnp.arange(tc)
    src = (i % SUBLANES) * seg + i // SUBLANES
    perm = (src[:, None] == i[None, :]).astype(BF16)
    permt = perm.T
    tri = (i[None, :] < i[:, None]).astype(BF16)

    row2 = lambda a: a.reshape(1, -1).astype(F32)
    operands = [
        x, row2(norm_mix), w_in.astype(BF16), pool_w.astype(BF16), row2(pool_scale), perm, permt, wb,
        a_r, a_i, ap_r, ap_i, mk_r, mk_i, as_r, as_i, wc_re, wc_im,
        row2(d_skip), glu_w.astype(BF16), row2(glu_b), w_out.astype(BF16),
        row2(norm_ffn), wr, br, tri,
    ]

    def whole(a):
        nd = a.ndim
        return pl.BlockSpec(a.shape, lambda bi, ni, nd=nd: (0,) * nd)

    n_t = l // tc
    n_tiles = (b * l) // TILE_ROWS + N_CLASSES
    in_specs = [pl.BlockSpec((1, tc, d), lambda bi, ni: (bi, ni, 0))] + [whole(a) for a in operands[1:]]
    kern = functools.partial(_mixer_kernel, tc=tc, d_pool=d_pool, d_ssm=d_ssm, n_tiles=n_tiles)
    return pl.pallas_call(
        kern,
        grid=(b, n_t),
        in_specs=in_specs,
        out_specs=[
            pl.BlockSpec(memory_space=pl.ANY),
            pl.BlockSpec((1, 1, tc), lambda bi, ni: (bi * n_t + ni, 0, 0)),
            pl.BlockSpec((1, 1, LANES), lambda bi, ni: (bi * n_t + ni, 0, 0)),
            pl.BlockSpec((SUBLANES, LANES), lambda bi, ni: (0, 0)),
        ],
        out_shape=[
            jax.ShapeDtypeStruct((n_tiles * TILE_ROWS * SUBLANES, LANES), F32),
            jax.ShapeDtypeStruct((b * n_t, 1, tc), I32),
            jax.ShapeDtypeStruct((b * n_t, 1, LANES), I32),
            jax.ShapeDtypeStruct((SUBLANES, LANES), I32),
        ],
        scratch_shapes=[
            pltpu.VMEM((POOL_HIST + tc, d_pool), F32),
            pltpu.VMEM((POOL_HIST + tc, d_pool), F32),
            pltpu.VMEM((POOL_HIST + tc, d_pool - POOL_GROUP), F32),
            pltpu.VMEM((POOL_HIST + tc, d_pool - 2 * POOL_GROUP), F32),
            pltpu.VMEM((tc, n_state), F32),
            pltpu.VMEM((tc, n_state), F32),
            pltpu.VMEM((SUBLANES, n_state), F32),
            pltpu.VMEM((SUBLANES, n_state), F32),
            pltpu.VMEM((2, tc * SUBLANES, LANES), F32),
            pltpu.VMEM((SUBLANES, tc), I32),
            pltpu.SMEM((3 * LANES,), I32),
            pltpu.VMEM((SUBLANES, LANES), F32),
            pltpu.SemaphoreType.DMA((2,)),
            pltpu.SemaphoreType.DMA,
            pltpu.SemaphoreType.DMA,
        ],
        compiler_params=pltpu.CompilerParams(
            dimension_semantics=("arbitrary", "arbitrary"), vmem_limit_bytes=VMEM_LIMIT),
        name="mixer",
    )(*operands)


def _experts_kernel(alloc_ref, fin_ref, x_ref, nf_ref, wr_ref, br_ref, wg_hbm, wu_hbm, wd_hbm,
                    nfin_ref, o_ref, wg_ref, wu_ref, wd_ref, tcls, tnv, wsem, *, n_steps):
    i = pl.program_id(0)
    n_tiles = pl.num_programs(0)
    rows = TILE_ROWS

    @pl.when(i == 0)
    def _():
        copies = [pltpu.make_async_copy(src, dst, wsem.at[j])
                  for j, (src, dst) in enumerate(((wg_hbm, wg_ref), (wu_hbm, wu_ref), (wd_hbm, wd_ref)))]
        for cp in copies:
            cp.start()
        free = fin_ref[2 * LANES]

        def init(k, carry):
            tcls[k] = jnp.minimum(k, N_CLASSES - 1)
            tnv[k] = jnp.where(k < free, rows, 0)
            return carry

        lax.fori_loop(0, n_tiles, init, 0)

        def log(s, carry):
            for c in range(N_CLASSES):
                t = alloc_ref[s * LANES + c]
                tcls[jnp.where(t >= 0, t, n_tiles)] = c
            return carry

        lax.fori_loop(0, n_steps, log, 0)
        for c in range(N_CLASSES):
            tnv[fin_ref[c]] = fin_ref[LANES + c]
        for cp in copies:
            cp.wait()

    @pl.when(tnv[i] == 0)
    def _():
        o_ref[...] = jnp.zeros_like(o_ref)

    @pl.when(tnv[i] > 0)
    def _():
        c = tcls[i]
        g = (c >= PAIRS_PER_GROUP).astype(I32) + (c >= 2 * PAIRS_PER_GROUP).astype(I32) \
            + (c >= 3 * PAIRS_PER_GROUP).astype(I32)
        p = c - g * PAIRS_PER_GROUP
        a = (p >= 3).astype(I32) + (p >= 5).astype(I32)
        b = p - ((a * (7 - a)) >> 1) + a + 1
        ea = g * EXPERTS_PER_GROUP + a
        eb = g * EXPERTS_PER_GROUP + b
        x = jnp.concatenate([x_ref[pl.ds(k, rows, stride=SUBLANES), :] for k in range(SUBLANES)], axis=-1)
        hb = _rms(x, nf_ref[...]).astype(BF16)
        logits = _dot(hb, wr_ref[...]) + br_ref[...]
        lane = lax.broadcasted_iota(I32, (rows, ROUTER_LANES), 1)
        is_c = lane < N_EXPERT_GROUPS
        cmax = jnp.max(jnp.where(is_c, logits, NEG), axis=-1, keepdims=True)
        psum = jnp.sum(jnp.where(is_c, jnp.exp(logits - cmax), 0.0), axis=-1, keepdims=True)
        cg = jnp.sum(jnp.where(lane == g, logits, 0.0), axis=-1, keepdims=True)
        p_g = jnp.exp(cg - cmax) / psum
        va = jnp.sum(jnp.where(lane == N_EXPERT_GROUPS + ea, logits, 0.0), axis=-1, keepdims=True)
        vb = jnp.sum(jnp.where(lane == N_EXPERT_GROUPS + eb, logits, 0.0), axis=-1, keepdims=True)
        vmax = jnp.maximum(va, vb)
        exa = jnp.exp(va - vmax)
        exb = jnp.exp(vb - vmax)
        den = exa + exb
        y = jnp.zeros_like(x)
        for e, w in ((ea, p_g * (exa / den)), (eb, p_g * (exb / den))):
            act = jax.nn.silu(_dot(hb, wg_ref[e])) * _dot(hb, wu_ref[e])
            y = y + _dot((act * w).astype(BF16), wd_ref[e])
        out = _rms(x + y, nfin_ref[...])
        for k in range(SUBLANES):
            o_ref[pl.ds(k, rows, stride=SUBLANES), :] = out[:, k * LANES:(k + 1) * LANES]


def _experts(xs, alloc, fin, norm_ffn, wr, br, w_gate, w_up, w_down, norm_final):
    n_e, d, f = w_gate.shape
    assert n_e == N_EXPERTS
    blk = TILE_ROWS * SUBLANES
    n_tiles = xs.shape[0] // blk
    n_steps = alloc.shape[0] // LANES
    row2 = lambda a: a.reshape(1, -1).astype(F32)
    const = lambda shape: pl.BlockSpec(shape, lambda i, al, fi: (0,) * len(shape))
    anyspec = pl.BlockSpec(memory_space=pl.ANY)
    grid_spec = pltpu.PrefetchScalarGridSpec(
        num_scalar_prefetch=2,
        grid=(n_tiles,),
        in_specs=[
            pl.BlockSpec((blk, LANES), lambda i, al, fi: (i, 0)),
            const((1, d)), const((d, ROUTER_LANES)), const((1, ROUTER_LANES)),
            anyspec, anyspec, anyspec,
            const((1, d)),
        ],
        out_specs=pl.BlockSpec((blk, LANES), lambda i, al, fi: (i, 0)),
        scratch_shapes=[
            pltpu.VMEM((n_e, d, f), BF16),
            pltpu.VMEM((n_e, d, f), BF16),
            pltpu.VMEM((n_e, f, d), BF16),
            pltpu.SMEM((n_tiles + 1,), I32),
            pltpu.SMEM((n_tiles + 1,), I32),
            pltpu.SemaphoreType.DMA((3,)),
        ],
    )
    return pl.pallas_call(
        functools.partial(_experts_kernel, n_steps=n_steps),
        grid_spec=grid_spec,
        out_shape=jax.ShapeDtypeStruct(xs.shape, F32),
        compiler_params=pltpu.CompilerParams(
            dimension_semantics=("arbitrary",), vmem_limit_bytes=VMEM_LIMIT),
        name="experts",
    )(alloc, fin, xs, row2(norm_ffn), wr, br,
      w_gate.astype(BF16), w_up.astype(BF16), w_down.astype(BF16), row2(norm_final))


def _unsort_kernel(dst_hbm, ys_hbm, o_ref, idx, buf, isem, gsem):
    i = pl.program_id(0)
    nt = pl.num_programs(0)
    slot = lax.rem(i, 2)
    rows = o_ref.shape[0]

    def idx_copy(tile, s3):
        return pltpu.make_async_copy(dst_hbm.at[tile], idx.at[pl.ds(s3 * rows, rows)], isem.at[s3])

    def gather_start(s3, bslot):
        def body(r2, c):
            for prio in range(2):
                r = r2 * 2 + prio
                src = idx[s3 * rows + r]
                pltpu.make_async_copy(ys_hbm.at[pl.ds(pl.multiple_of(src * SUBLANES, SUBLANES), SUBLANES)],
                                      buf.at[bslot, pl.ds(pl.multiple_of(r * SUBLANES, SUBLANES), SUBLANES)],
                                      gsem.at[bslot]).start(priority=prio)
            return c

        lax.fori_loop(0, rows // 2, body, 0, unroll=4)

    @pl.when(i == 0)
    def _():
        idx_copy(0, 0).start()
        idx_copy(0, 0).wait()

        @pl.when(nt > 1)
        def _():
            idx_copy(1, 1).start()
            idx_copy(1, 1).wait()

        gather_start(0, 0)

    @pl.when(i + 2 < nt)
    def _():
        idx_copy(i + 2, lax.rem(i + 2, 3)).start()

    @pl.when((i >= 1) & (i + 1 < nt))
    def _():
        idx_copy(i + 1, lax.rem(i + 1, 3)).wait()

    @pl.when(i + 1 < nt)
    def _():
        gather_start(lax.rem(i + 1, 3), 1 - slot)

    pltpu.make_async_copy(ys_hbm.at[pl.ds(0, rows * SUBLANES)], buf.at[slot], gsem.at[slot]).wait()
    o_ref[...] = jnp.concatenate(
        [buf[slot, pl.ds(c, rows, stride=SUBLANES), :] for c in range(SUBLANES)], axis=-1)


def _unsort(ys, dst, n_tok, d):
    n_tiles, _, rows = dst.shape
    anyspec = pl.BlockSpec(memory_space=pl.ANY)
    return pl.pallas_call(
        _unsort_kernel,
        grid=(n_tiles,),
        in_specs=[anyspec, anyspec],
        out_specs=pl.BlockSpec((rows, d), lambda i: (i, 0)),
        out_shape=jax.ShapeDtypeStruct((n_tok, d), F32),
        scratch_shapes=[
            pltpu.SMEM((3 * rows,), I32),
            pltpu.VMEM((2, rows * SUBLANES, LANES), F32),
            pltpu.SemaphoreType.DMA((3,)),
            pltpu.SemaphoreType.DMA((2,)),
        ],
        compiler_params=pltpu.CompilerParams(
            dimension_semantics=("arbitrary",), vmem_limit_bytes=VMEM_LIMIT),
        name="unsort",
    )(dst.reshape(n_tiles, rows), ys)


def kernel(x, norm_mix, w_in, pool_w, pool_scale, ssm_a_re, ssm_a_im, ssm_log_step, ssm_b_re, ssm_b_im, ssm_c_re, ssm_c_im, ssm_d, glu_w, glu_b, w_out, norm_ffn, router_coarse_w, router_coarse_b, router_fine_w, router_fine_b, exp_w_gate, exp_w_up, exp_w_down, norm_final):
    assert norm_mix.shape[0] == 1, "the experts call fuses the final norm: single-layer blocks only"
    b, l, d = x.shape
    wr, br = _router_weights(router_coarse_w[0], router_coarse_b[0], router_fine_w[0], router_fine_b[0])
    xs, dst, alloc, fin = _mixer(
        x, norm_mix[0], w_in[0], pool_w[0], pool_scale[0], ssm_a_re[0], ssm_a_im[0], ssm_log_step[0],
        ssm_b_re[0], ssm_b_im[0], ssm_c_re[0], ssm_c_im[0], ssm_d[0], glu_w[0], glu_b[0], w_out[0],
        norm_ffn[0], wr, br)
    ys = _experts(xs, alloc.reshape(-1), fin.reshape(-1), norm_ffn[0], wr, br,
                  exp_w_gate[0], exp_w_up[0], exp_w_down[0], norm_final)
    y = _unsort(ys, dst, b * l, d)
    return y.reshape(b, l, d)
```

```python
import functools

import jax
import jax.numpy as jnp
from jax import lax
from jax.experimental import pallas as pl
from jax.experimental.pallas import tpu as pltpu

EPS = 1e-6
POOL_WINDOWS = (2, 4, 8, 16)
POOL_GROUP = 128
POOL_TAIL = 16
POOL_HIST = 32
SSM_GROUP = 16
SSM_STATE = 64
N_EXPERT_GROUPS = 4
EXPERTS_PER_GROUP = 4
N_EXPERTS = 16
PAIRS_PER_GROUP = 6
N_CLASSES = N_EXPERT_GROUPS * PAIRS_PER_GROUP

SUBLANES = 8
LANES = 128
SLAB_GROUPS = 8
STATE_SLAB = SLAB_GROUPS * SSM_STATE
SCAN_LANES = 1024
TIME_TILE = 512
SCAN_TILE = 256
TILE_ROWS = 512
TILE_ROWS_LOG2 = 9
BF16_EXACT = 256
ROUTER_LANES = 128
VMEM_LIMIT = 56 * 1024 * 1024
NEG = -3.0e38

F32 = jnp.float32
BF16 = jnp.bfloat16
I32 = jnp.int32
NT_DIMS = (((1,), (1,)), ((), ()))


def _dot(a, b):
    return jnp.dot(a, b, preferred_element_type=F32)


def _dot_nt(a, b):
    return lax.dot_general(a, b, NT_DIMS, preferred_element_type=F32)


def _rms(x, g):
    return x * lax.rsqrt(jnp.mean(x * x, axis=-1, keepdims=True) + EPS) * g


def _rows8(*rows):
    r8 = lax.broadcasted_iota(I32, (SUBLANES, LANES), 0)
    out = jnp.zeros((SUBLANES, LANES), F32)
    for k, row in enumerate(rows):
        out = jnp.where(r8 == k, jnp.broadcast_to(row, (SUBLANES, LANES)), out)
    return out


def _route(logits, lane):
    is_c = lane < N_EXPERT_GROUPS
    cm = jnp.where(is_c, logits, NEG)
    cmax = jnp.max(cm, axis=-1, keepdims=True)
    gi = jnp.min(jnp.where(cm == cmax, lane, ROUTER_LANES), axis=-1, keepdims=True)
    lo = N_EXPERT_GROUPS + gi * EXPERTS_PER_GROUP
    insel = (lane >= lo) & (lane < lo + EXPERTS_PER_GROUP)
    fs = jnp.where(insel, logits, NEG)
    v0 = jnp.max(fs, axis=-1, keepdims=True)
    i0 = jnp.min(jnp.where(insel & (fs == v0), lane, ROUTER_LANES), axis=-1, keepdims=True)
    rest = insel & (lane != i0)
    fs2 = jnp.where(rest, logits, NEG)
    v1 = jnp.max(fs2, axis=-1, keepdims=True)
    i1 = jnp.min(jnp.where(rest & (fs2 == v1), lane, ROUTER_LANES), axis=-1, keepdims=True)
    return gi, lo, i0, i1


def _mixer_kernel(x_ref, nm_ref, win_ref, pw_ref, ps_ref, perm_ref, permt_ref, wb_ref,
                  are_ref, aim_ref, apre_ref, apim_ref, mkre_ref, mkim_ref, asre_ref, asim_ref,
                  wcre_ref, wcim_ref, d_ref, gw_ref, gb_ref, wout_ref, nf_ref, wr_ref, br_ref, tri_ref,
                  xs_hbm, dst_ref, alloc_ref, fin_ref,
                  zpad, lv2, lv4, lv8, bur, bui, st_r, st_i, x1t, meta_v, meta_s, alloc_v, ssem, msem, zsem,
                  *, tc, d_pool, d_ssm, n_tiles):
    bq = pl.program_id(0)
    n = pl.program_id(1)
    n_t = pl.num_programs(1)
    step = bq * n_t + n
    n_steps = pl.num_programs(0) * n_t
    slot = lax.rem(step, 2)
    seg = SCAN_TILE // SUBLANES
    n_sub = tc // SCAN_TILE
    n_slab = d_ssm // LANES
    n_chunk = n_slab * STATE_SLAB // SCAN_LANES
    per_iter = tc // (2 * n_chunk * seg * n_sub)
    assert per_iter * 2 * n_chunk * seg * n_sub == tc
    lane1 = lax.broadcasted_iota(I32, (1, LANES), 1)

    @pl.when(n == 0)
    def _():
        zpad[0:POOL_HIST, :] = jnp.zeros((POOL_HIST, d_pool), F32)
        st_r[...] = jnp.zeros_like(st_r)
        st_i[...] = jnp.zeros_like(st_i)

    @pl.when(step == 0)
    def _():
        lane_f = lane1.astype(F32)
        alloc_v[...] = _rows8(jnp.where(lane1 < N_CLASSES, lane_f, 0.0), jnp.zeros((1, LANES), F32),
                              jnp.full((1, LANES), float(N_CLASSES), F32))

    meta_cp = pltpu.make_async_copy(meta_v.at[0], meta_s.at[pl.ds(0, tc)], msem)

    def sent(s_):
        return pltpu.make_async_copy(x1t.at[s_], xs_hbm.at[pl.ds(0, tc * SUBLANES)], ssem.at[s_])

    def send_row(s_, r, prio):
        d = meta_s[r]
        pltpu.make_async_copy(x1t.at[s_, pl.ds(pl.multiple_of(r * SUBLANES, SUBLANES), SUBLANES)],
                              xs_hbm.at[pl.ds(pl.multiple_of(d * SUBLANES, SUBLANES), SUBLANES)],
                              ssem.at[s_]).start(priority=prio)

    def send_rows(s_):
        def send(r2, carry):
            for prio in range(2):
                send_row(s_, r2 * 2 + prio, prio)
            return carry

        lax.fori_loop(0, tc // 2, send, 0, unroll=4)

    def send_some(r_first):
        for k in range(per_iter):
            send_row(1 - slot, r_first + k, k % 2)

    x = x_ref[0]
    hn = _rms(x, nm_ref[...]).astype(BF16)
    z = _dot(hn, win_ref[...])

    zp = z[:, :d_pool]
    top = POOL_HIST + tc
    zpad[POOL_HIST:top, :] = zp
    g1, g2, g3 = POOL_GROUP, 2 * POOL_GROUP, 3 * POOL_GROUP
    lv2[8:top, :] = zpad[8:top, :] + zpad[7:top - 1, :]
    lv4[16:top, :] = lv2[16:top, g1:] + lv2[14:top - 2, g1:]
    lv8[24:top, :] = lv4[24:top, g1:] + lv4[20:top - 4, g1:]
    wins = (lv2[POOL_HIST:top, 0:g1], lv4[POOL_HIST:top, 0:g1], lv8[POOL_HIST:top, 0:g1],
            lv8[POOL_HIST:top, g1:] + lv8[POOL_HIST - 8:top - 8, g1:])
    t = n * tc + lax.broadcasted_iota(I32, (tc, 1), 0)
    pooled = []
    for gi, w in enumerate(POOL_WINDOWS):
        cols = slice(gi * POOL_GROUP, (gi + 1) * POOL_GROUP)
        cnt = jnp.minimum(t + 1, w).astype(F32)
        p = wins[gi] / cnt - zp[:, cols]
        pooled.append(_dot(p.astype(BF16), pw_ref[gi]))
    y_pool = jnp.concatenate(pooled, axis=-1) * ps_ref[...]
    zpad[POOL_HIST - POOL_TAIL:POOL_HIST, :] = zpad[top - POOL_TAIL:top, :]

    u = z[:, d_pool:]
    ub = u.astype(BF16)
    up = jnp.concatenate([_dot(perm_ref[...], ub[h * SCAN_TILE:(h + 1) * SCAN_TILE]) for h in range(n_sub)],
                         axis=0)
    upb = up.astype(BF16)
    for m in range(n_slab):
        r = _dot(upb[:, m * LANES:(m + 1) * LANES], wb_ref[m])
        bur[:, m * STATE_SLAB:(m + 1) * STATE_SLAB] = r[:, :STATE_SLAB]
        bui[:, m * STATE_SLAB:(m + 1) * STATE_SLAB] = r[:, STATE_SLAB:]

    def scan_all(sending):
        row = lax.broadcasted_iota(I32, (SUBLANES, SCAN_LANES), 0)
        for h in range(n_sub):
            for c in range(n_chunk):
                scan_chunk(h, c, row, sending)

    def scan_chunk(h, c, row, sending):
        cs = slice(c * SCAN_LANES, (c + 1) * SCAN_LANES)
        row0 = h * SCAN_TILE
        loop0 = (h * n_chunk + c) * 2 * seg
        ar = are_ref[:, cs]
        ai = aim_ref[:, cs]

        def scan_body(s, carry):
            xr, xi = carry
            r0 = pl.multiple_of(row0 + s * SUBLANES, SUBLANES)
            nxr = ar * xr - ai * xi + bur[pl.ds(r0, SUBLANES), cs]
            nxi = ar * xi + ai * xr + bui[pl.ds(r0, SUBLANES), cs]
            bur[pl.ds(r0, SUBLANES), cs] = nxr
            bui[pl.ds(r0, SUBLANES), cs] = nxi
            if sending:
                send_some((loop0 + s) * per_iter)
            return nxr, nxi

        zero = jnp.zeros((SUBLANES, SCAN_LANES), F32)
        er, ei = lax.fori_loop(0, seg, scan_body, (zero, zero), unroll=4)

        fr = jnp.where(row == 0, st_r[:, cs], pltpu.roll(er, 1, 0))
        fi = jnp.where(row == 0, st_i[:, cs], pltpu.roll(ei, 1, 0))
        for ki, k in enumerate((1, 2, 4)):
            mr = mkre_ref[ki, :, cs]
            mi = mkim_ref[ki, :, cs]
            rr = pltpu.roll(fr, k, 0)
            ri = pltpu.roll(fi, k, 0)
            fr, fi = fr + mr * rr - mi * ri, fi + mr * ri + mi * rr
        asr = asre_ref[:, cs]
        asi = asim_ref[:, cs]
        nsr = asr * fr - asi * fi + er
        nsi = asr * fi + asi * fr + ei
        st_r[:, cs] = pltpu.roll(nsr, 1, 0)
        st_i[:, cs] = pltpu.roll(nsi, 1, 0)

        def fix_body(s, carry):
            t0 = pl.multiple_of(s * SUBLANES, SUBLANES)
            r0 = pl.multiple_of(row0 + s * SUBLANES, SUBLANES)
            pr = apre_ref[pl.ds(t0, SUBLANES), cs]
            pi = apim_ref[pl.ds(t0, SUBLANES), cs]
            bur[pl.ds(r0, SUBLANES), cs] = bur[pl.ds(r0, SUBLANES), cs] + pr * fr - pi * fi
            bui[pl.ds(r0, SUBLANES), cs] = bui[pl.ds(r0, SUBLANES), cs] + pr * fi + pi * fr
            if sending:
                send_some((loop0 + seg + s) * per_iter)
            return carry

        lax.fori_loop(0, seg, fix_body, 0, unroll=4)

    @pl.when(step == 0)
    def _():
        scan_all(False)

    @pl.when(step >= 1)
    def _():
        meta_cp.wait()
        scan_all(True)

    ys = []
    for m in range(n_slab):
        cs = slice(m * STATE_SLAB, (m + 1) * STATE_SLAB)
        ys.append(_dot(bur[:, cs].astype(BF16), wcre_ref[m]) + _dot(bui[:, cs].astype(BF16), wcim_ref[m]))
    y = jnp.concatenate(ys, axis=-1) + d_ref[...] * up
    y = jax.nn.gelu(y)
    y = y * jax.nn.sigmoid(_dot(y.astype(BF16), gw_ref[...]) + gb_ref[...])
    yb = y.astype(BF16)
    y_ssm = jnp.concatenate([_dot(permt_ref[...], yb[h * SCAN_TILE:(h + 1) * SCAN_TILE]) for h in range(n_sub)],
                            axis=0)

    mix = _dot(y_pool.astype(BF16), wout_ref[0:d_pool, :]) + _dot(y_ssm.astype(BF16), wout_ref[d_pool:, :])
    x1 = x + mix

    hb = _rms(x1, nf_ref[...]).astype(BF16)
    logits = _dot(hb, wr_ref[...]) + br_ref[...]
    lane = lax.broadcasted_iota(I32, (tc, ROUTER_LANES), 1)
    gi, lo, i0, i1 = _route(logits, lane)
    ea = jnp.minimum(i0, i1) - lo
    eb = jnp.maximum(i0, i1) - lo
    cls = gi * PAIRS_PER_GROUP + ((ea * (7 - ea)) >> 1) + (eb - ea - 1)
    oh = (lane == cls).astype(F32)
    ohb = oh.astype(BF16)
    before = _dot(tri_ref[...], ohb) * oh
    ones8 = jnp.ones((SUBLANES, ROUTER_LANES), BF16)
    before_hi = jnp.floor(before * (1.0 / BF16_EXACT))
    before_lo = before - before_hi * float(BF16_EXACT)
    rank = (_dot_nt(ones8, before_hi.astype(BF16))[0:1, :] * float(BF16_EXACT)
            + _dot_nt(ones8, before_lo.astype(BF16))[0:1, :])
    cnt = jnp.sum(oh, axis=0, keepdims=True)

    st = alloc_v[...]
    open_t, fill, free = st[0:1], st[1:2], st[2:3]
    room = float(TILE_ROWS) - fill
    spill = cnt > room
    spill_f = spill.astype(F32)
    upper = (lax.broadcasted_iota(I32, (LANES, LANES), 0) < lax.broadcasted_iota(I32, (LANES, LANES), 1))
    earlier = _dot(_rows8(spill_f).astype(BF16), upper.astype(F32).astype(BF16))[0:1]
    new_t = free + earlier
    fill_hi = jnp.floor(fill * (1.0 / BF16_EXACT))
    per_row = _dot_nt(_rows8(open_t, new_t, fill_hi, fill - fill_hi * float(BF16_EXACT)).astype(BF16), ohb)
    open_r, new_r = per_row[0:1], per_row[1:2]
    fill_r = per_row[2:3] * float(BF16_EXACT) + per_row[3:4]
    room_r = float(TILE_ROWS) - fill_r
    dst = jnp.where(rank < room_r, open_r * float(TILE_ROWS) + fill_r + rank,
                    new_r * float(TILE_ROWS) + rank - room_r).astype(I32)
    dst_ref[0] = dst
    alloc_ref[0] = jnp.where(spill, new_t, -1.0).astype(I32)
    new_state = _rows8(jnp.where(spill, new_t, open_t), jnp.where(spill, cnt - room, fill + cnt),
                       free + jnp.sum(spill_f, axis=-1, keepdims=True))
    alloc_v[...] = new_state
    fin_ref[...] = new_state.astype(I32)

    @pl.when(step >= 2)
    def _():
        sent(slot).wait()

    for c in range(SUBLANES):
        x1t[slot, pl.ds(c, tc, stride=SUBLANES), :] = x1[:, c * LANES:(c + 1) * LANES]
    meta_v[0:1, :] = dst
    meta_cp.start()

    @pl.when(step == n_steps - 1)
    def _():
        meta_cp.wait()
        send_rows(slot)

        @pl.when(step >= 1)
        def _():
            sent(1 - slot).wait()


        zsrc = x1t.at[1 - slot]
        zsrc[...] = jnp.zeros((tc * SUBLANES, LANES), F32)
        meta_v[:, 0:LANES] = new_state.astype(I32)
        for k in range(3):
            cp = pltpu.make_async_copy(meta_v.at[k, pl.ds(0, LANES)], meta_s.at[pl.ds(k * LANES, LANES)], msem)
            cp.start()
            cp.wait()

        def zcopy(first_row, n_rows):
            return pltpu.make_async_copy(
                zsrc.at[pl.ds(0, n_rows * SUBLANES)],
                xs_hbm.at[pl.ds(pl.multiple_of(first_row * SUBLANES, SUBLANES), n_rows * SUBLANES)], zsem)

        for c in range(N_CLASSES):
            tile = meta_s[c]
            fill_c = meta_s[LANES + c]
            rem = TILE_ROWS - fill_c
            for bit in range(TILE_ROWS_LOG2, -1, -1):
                size = 1 << bit
                first = (tile << TILE_ROWS_LOG2) + fill_c + (rem & ~(2 * size - 1))

                @pl.when((rem & size) != 0)
                def _(first=first, size=size):
                    cp = zcopy(first, size)
                    cp.start()
                    cp.wait()

        def tail(k, carry):
            cp = zcopy(k << TILE_ROWS_LOG2, TILE_ROWS)
            cp.start()
            cp.wait()
            return carry

        lax.fori_loop(meta_s[2 * LANES], n_tiles, tail, 0)
        sent(slot).wait()


def _ssm_tables(a_re, a_im, log_step, b_re, b_im, c_re, c_im, seg):
    g, p = a_re.shape
    h = b_re.shape[-1]
    n_slab = g // SLAB_GROUPS
    lr, li = a_re.astype(F32), a_im.astype(F32)
    step = jnp.exp(log_step.astype(F32))[:, None]
    mag = jnp.exp(lr * step)
    ab_re = mag * jnp.cos(li * step)
    ab_im = mag * jnp.sin(li * step)
    den = lr * lr + li * li
    nr, ni = ab_re - 1.0, ab_im
    q_re = (nr * lr + ni * li) / den
    q_im = (ni * lr - nr * li) / den
    br, bi = b_re.astype(F32), b_im.astype(F32)
    bb_re = q_re[..., None] * br - q_im[..., None] * bi
    bb_im = q_re[..., None] * bi + q_im[..., None] * br

    eye = jnp.eye(SLAB_GROUPS, dtype=F32)

    def b_slab(bb):
        bb = bb.reshape(n_slab, SLAB_GROUPS, p, h)
        w = jnp.einsum('mgph,gk->mghkp', bb, eye)
        return w.reshape(n_slab, SLAB_GROUPS * h, SLAB_GROUPS * p)

    wb = jnp.concatenate([b_slab(bb_re), b_slab(bb_im)], axis=-1).astype(BF16)

    def c_slab(cc):
        cc = cc.astype(F32).reshape(n_slab, SLAB_GROUPS, h, p)
        w = jnp.einsum('mghp,gk->mgpkh', cc, eye)
        return w.reshape(n_slab, SLAB_GROUPS * p, SLAB_GROUPS * h)

    wc_re = c_slab(c_re).astype(BF16)
    wc_im = c_slab(-c_im).astype(BF16)

    def apow(nn):
        nn = nn.astype(F32)[:, None, None]
        m_ = jnp.exp(nn * (lr * step))
        ang = nn * (li * step)
        return (m_ * jnp.cos(ang)).reshape(-1, g * p), (m_ * jnp.sin(ang)).reshape(-1, g * p)

    a1r, a1i = apow(jnp.ones((1,), F32))
    a_r = jnp.broadcast_to(a1r, (SUBLANES, g * p))
    a_i = jnp.broadcast_to(a1i, (SUBLANES, g * p))
    pr, pi = apow(jnp.arange(1, seg + 1))
    ap_r = jnp.repeat(pr, SUBLANES, axis=0)
    ap_i = jnp.repeat(pi, SUBLANES, axis=0)
    ks = jnp.array([1, 2, 4])
    mr, mi = apow(ks * seg)
    rows = jnp.arange(SUBLANES)[None, :, None]
    mask = (rows >= ks[:, None, None]).astype(F32)
    mk_r = mr[:, None, :] * mask
    mk_i = mi[:, None, :] * mask
    sr, si = apow(jnp.full((1,), seg, F32))
    as_r = jnp.broadcast_to(sr, (SUBLANES, g * p))
    as_i = jnp.broadcast_to(si, (SUBLANES, g * p))
    return wb, a_r, a_i, ap_r, ap_i, mk_r, mk_i, as_r, as_i, wc_re, wc_im


def _router_weights(w_coarse, b_coarse, w_fine, b_fine):
    d = w_coarse.shape[0]
    pad = ROUTER_LANES - N_EXPERT_GROUPS - N_EXPERTS
    wr = jnp.concatenate([w_coarse, w_fine, jnp.zeros((d, pad), F32)], axis=-1).astype(BF16)
    br = jnp.concatenate([b_coarse, b_fine, jnp.zeros((pad,), F32)]).reshape(1, -1).astype(F32)
    return wr, br


def _mixer(x, norm_mix, w_in, pool_w, pool_scale, a_re, a_im, log_step, b_re, b_im, c_re, c_im,
           d_skip, glu_w, glu_b, w_out, norm_ffn, wr, br):
    b, l, d = x.shape
    d_pool = pool_scale.shape[-1]
    d_ssm = d_skip.shape[-1]
    tc = TIME_TILE
    assert l % tc == 0 and tc % SCAN_TILE == 0
    assert d_pool == len(POOL_WINDOWS) * POOL_GROUP and d_ssm % LANES == 0
    assert d == SUBLANES * LANES, "one token row must fill exactly one (8, 128) tile"
    assert tc == TILE_ROWS, "a mixer tile must never need more than one new sorted tile per class"
    assert N_CLASSES <= LANES
    seg = SCAN_TILE // SUBLANES
    n_state = a_re.shape[0] * a_re.shape[1]
    assert (b * l) // TILE_ROWS + N_CLASSES <= BF16_EXACT, "tile indices pass through bf16 operands"
    tables = _ssm_tables(a_re, a_im, log_step, b_re, b_im, c_re, c_im, seg)
    wb, a_r, a_i, ap_r, ap_i, mk_r, mk_i, as_r, as_i, wc_re, wc_im = tables

    i = jnp.arange(SCAN_TILE)
    src = (i % SUBLANES) * seg + i // SUBLANES
    perm = (src[:, None] == i[None, :]).astype(BF16)
    permt = perm.T
    it = jnp.arange(tc)
    tri = (it[None, :] < it[:, None]).astype(BF16)

    row2 = lambda a: a.reshape(1, -1).astype(F32)
    operands = [
        x, row2(norm_mix), w_in.astype(BF16), pool_w.astype(BF16), row2(pool_scale), perm, permt, wb,
        a_r, a_i, ap_r, ap_i, mk_r, mk_i, as_r, as_i, wc_re, wc_im,
        row2(d_skip), glu_w.astype(BF16), row2(glu_b), w_out.astype(BF16),
        row2(norm_ffn), wr, br, tri,
    ]

    def whole(a):
        nd = a.ndim
        return pl.BlockSpec(a.shape, lambda bi, ni, nd=nd: (0,) * nd)

    n_t = l // tc
    n_tiles = (b * l) // TILE_ROWS + N_CLASSES
    in_specs = [pl.BlockSpec((1, tc, d), lambda bi, ni: (bi, ni, 0))] + [whole(a) for a in operands[1:]]
    kern = functools.partial(_mixer_kernel, tc=tc, d_pool=d_pool, d_ssm=d_ssm, n_tiles=n_tiles)
    return pl.pallas_call(
        kern,
        grid=(b, n_t),
        in_specs=in_specs,
        out_specs=[
            pl.BlockSpec(memory_space=pl.ANY),
            pl.BlockSpec((1, 1, tc), lambda bi, ni: (bi * n_t + ni, 0, 0)),
            pl.BlockSpec((1, 1, LANES), lambda bi, ni: (bi * n_t + ni, 0, 0)),
            pl.BlockSpec((SUBLANES, LANES), lambda bi, ni: (0, 0)),
        ],
        out_shape=[
            jax.ShapeDtypeStruct((n_tiles * TILE_ROWS * SUBLANES, LANES), F32),
            jax.ShapeDtypeStruct((b * n_t, 1, tc), I32),
            jax.ShapeDtypeStruct((b * n_t, 1, LANES), I32),
            jax.ShapeDtypeStruct((SUBLANES, LANES), I32),
        ],
        scratch_shapes=[
            pltpu.VMEM((POOL_HIST + tc, d_pool), F32),
            pltpu.VMEM((POOL_HIST + tc, d_pool), F32),
            pltpu.VMEM((POOL_HIST + tc, d_pool - POOL_GROUP), F32),
            pltpu.VMEM((POOL_HIST + tc, d_pool - 2 * POOL_GROUP), F32),
            pltpu.VMEM((tc, n_state), F32),
            pltpu.VMEM((tc, n_state), F32),
            pltpu.VMEM((SUBLANES, n_state), F32),
            pltpu.VMEM((SUBLANES, n_state), F32),
            pltpu.VMEM((2, tc * SUBLANES, LANES), F32),
            pltpu.VMEM((SUBLANES, tc), I32),
            pltpu.SMEM((max(tc, 3 * LANES),), I32),
            pltpu.VMEM((SUBLANES, LANES), F32),
            pltpu.SemaphoreType.DMA((2,)),
            pltpu.SemaphoreType.DMA,
            pltpu.SemaphoreType.DMA,
        ],
        compiler_params=pltpu.CompilerParams(
            dimension_semantics=("arbitrary", "arbitrary"), vmem_limit_bytes=VMEM_LIMIT),
        name="mixer",
    )(*operands)


def _experts_kernel(alloc_ref, fin_ref, x_ref, nf_ref, wr_ref, br_ref, wg_hbm, wu_hbm, wd_hbm,
                    nfin_ref, o_ref, wg_ref, wu_ref, wd_ref, tcls, tnv, wsem, *, n_steps):
    i = pl.program_id(0)
    n_tiles = pl.num_programs(0)
    rows = TILE_ROWS

    @pl.when(i == 0)
    def _():
        copies = [pltpu.make_async_copy(src, dst, wsem.at[j])
                  for j, (src, dst) in enumerate(((wg_hbm, wg_ref), (wu_hbm, wu_ref), (wd_hbm, wd_ref)))]
        for cp in copies:
            cp.start()
        free = fin_ref[2 * LANES]

        def init(k, carry):
            tcls[k] = jnp.minimum(k, N_CLASSES - 1)
            tnv[k] = jnp.where(k < free, rows, 0)
            return carry

        lax.fori_loop(0, n_tiles, init, 0)

        def log(s, carry):
            for c in range(N_CLASSES):
                t = alloc_ref[s * LANES + c]
                tcls[jnp.where(t >= 0, t, n_tiles)] = c
            return carry

        lax.fori_loop(0, n_steps, log, 0)
        for c in range(N_CLASSES):
            tnv[fin_ref[c]] = fin_ref[LANES + c]
        for cp in copies:
            cp.wait()

    @pl.when(tnv[i] == 0)
    def _():
        o_ref[...] = jnp.zeros_like(o_ref)

    @pl.when(tnv[i] > 0)
    def _():
        c = tcls[i]
        g = (c >= PAIRS_PER_GROUP).astype(I32) + (c >= 2 * PAIRS_PER_GROUP).astype(I32) \
            + (c >= 3 * PAIRS_PER_GROUP).astype(I32)
        p = c - g * PAIRS_PER_GROUP
        a = (p >= 3).astype(I32) + (p >= 5).astype(I32)
        b = p - ((a * (7 - a)) >> 1) + a + 1
        ea = g * EXPERTS_PER_GROUP + a
        eb = g * EXPERTS_PER_GROUP + b
        x = jnp.concatenate([x_ref[pl.ds(k, rows, stride=SUBLANES), :] for k in range(SUBLANES)], axis=-1)
        hb = _rms(x, nf_ref[...]).astype(BF16)
        logits = _dot(hb, wr_ref[...]) + br_ref[...]
        lane = lax.broadcasted_iota(I32, (rows, ROUTER_LANES), 1)
        is_c = lane < N_EXPERT_GROUPS
        cmax = jnp.max(jnp.where(is_c, logits, NEG), axis=-1, keepdims=True)
        psum = jnp.sum(jnp.where(is_c, jnp.exp(logits - cmax), 0.0), axis=-1, keepdims=True)
        cg = jnp.sum(jnp.where(lane == g, logits, 0.0), axis=-1, keepdims=True)
        p_g = jnp.exp(cg - cmax) / psum
        va = jnp.sum(jnp.where(lane == N_EXPERT_GROUPS + ea, logits, 0.0), axis=-1, keepdims=True)
        vb = jnp.sum(jnp.where(lane == N_EXPERT_GROUPS + eb, logits, 0.0), axis=-1, keepdims=True)
        vmax = jnp.maximum(va, vb)
        exa = jnp.exp(va - vmax)
        exb = jnp.exp(vb - vmax)
        den = exa + exb
        y = jnp.zeros_like(x)
        for e, w in ((ea, p_g * (exa / den)), (eb, p_g * (exb / den))):
            act = jax.nn.silu(_dot(hb, wg_ref[e])) * _dot(hb, wu_ref[e])
            y = y + _dot((act * w).astype(BF16), wd_ref[e])
        out = _rms(x + y, nfin_ref[...])
        for k in range(SUBLANES):
            o_ref[pl.ds(k, rows, stride=SUBLANES), :] = out[:, k * LANES:(k + 1) * LANES]


def _experts(xs, alloc, fin, norm_ffn, wr, br, w_gate, w_up, w_down, norm_final):
    n_e, d, f = w_gate.shape
    assert n_e == N_EXPERTS
    blk = TILE_ROWS * SUBLANES
    n_tiles = xs.shape[0] // blk
    n_steps = alloc.shape[0] // LANES
    row2 = lambda a: a.reshape(1, -1).astype(F32)
    const = lambda shape: pl.BlockSpec(shape, lambda i, al, fi: (0,) * len(shape))
    anyspec = pl.BlockSpec(memory_space=pl.ANY)
    grid_spec = pltpu.PrefetchScalarGridSpec(
        num_scalar_prefetch=2,
        grid=(n_tiles,),
        in_specs=[
            pl.BlockSpec((blk, LANES), lambda i, al, fi: (i, 0)),
            const((1, d)), const((d, ROUTER_LANES)), const((1, ROUTER_LANES)),
            anyspec, anyspec, anyspec,
            const((1, d)),
        ],
        out_specs=pl.BlockSpec((blk, LANES), lambda i, al, fi: (i, 0)),
        scratch_shapes=[
            pltpu.VMEM((n_e, d, f), BF16),
            pltpu.VMEM((n_e, d, f), BF16),
            pltpu.VMEM((n_e, f, d), BF16),
            pltpu.SMEM((n_tiles + 1,), I32),
            pltpu.SMEM((n_tiles + 1,), I32),
            pltpu.SemaphoreType.DMA((3,)),
        ],
    )
    return pl.pallas_call(
        functools.partial(_experts_kernel, n_steps=n_steps),
        grid_spec=grid_spec,
        out_shape=jax.ShapeDtypeStruct(xs.shape, F32),
        compiler_params=pltpu.CompilerParams(
            dimension_semantics=("arbitrary",), vmem_limit_bytes=VMEM_LIMIT),
        name="experts",
    )(alloc, fin, xs, row2(norm_ffn), wr, br,
      w_gate.astype(BF16), w_up.astype(BF16), w_down.astype(BF16), row2(norm_final))


def _unsort_kernel(dst_hbm, ys_hbm, o_ref, idx, buf, isem, gsem):
    i = pl.program_id(0)
    nt = pl.num_programs(0)
    slot = lax.rem(i, 2)
    rows = o_ref.shape[0]

    def idx_copy(tile, s3):
        return pltpu.make_async_copy(dst_hbm.at[tile], idx.at[pl.ds(s3 * rows, rows)], isem.at[s3])

    def gather_start(s3, bslot):
        def body(r2, c):
            for prio in range(2):
                r = r2 * 2 + prio
                src = idx[s3 * rows + r]
                pltpu.make_async_copy(ys_hbm.at[pl.ds(pl.multiple_of(src * SUBLANES, SUBLANES), SUBLANES)],
                                      buf.at[bslot, pl.ds(pl.multiple_of(r * SUBLANES, SUBLANES), SUBLANES)],
                                      gsem.at[bslot]).start(priority=prio)
            return c

        lax.fori_loop(0, rows // 2, body, 0, unroll=4)

    @pl.when(i == 0)
    def _():
        idx_copy(0, 0).start()
        idx_copy(0, 0).wait()

        @pl.when(nt > 1)
        def _():
            idx_copy(1, 1).start()
            idx_copy(1, 1).wait()

        gather_start(0, 0)

    @pl.when(i + 2 < nt)
    def _():
        idx_copy(i + 2, lax.rem(i + 2, 3)).start()

    @pl.when((i >= 1) & (i + 1 < nt))
    def _():
        idx_copy(i + 1, lax.rem(i + 1, 3)).wait()

    @pl.when(i + 1 < nt)
    def _():
        gather_start(lax.rem(i + 1, 3), 1 - slot)

    pltpu.make_async_copy(ys_hbm.at[pl.ds(0, rows * SUBLANES)], buf.at[slot], gsem.at[slot]).wait()
    o_ref[...] = jnp.concatenate(
        [buf[slot, pl.ds(c, rows, stride=SUBLANES), :] for c in range(SUBLANES)], axis=-1)


def _unsort(ys, dst, n_tok, d):
    n_tiles, _, rows = dst.shape
    anyspec = pl.BlockSpec(memory_space=pl.ANY)
    return pl.pallas_call(
        _unsort_kernel,
        grid=(n_tiles,),
        in_specs=[anyspec, anyspec],
        out_specs=pl.BlockSpec((rows, d), lambda i: (i, 0)),
        out_shape=jax.ShapeDtypeStruct((n_tok, d), F32),
        scratch_shapes=[
            pltpu.SMEM((3 * rows,), I32),
            pltpu.VMEM((2, rows * SUBLANES, LANES), F32),
            pltpu.SemaphoreType.DMA((3,)),
            pltpu.SemaphoreType.DMA((2,)),
        ],
        compiler_params=pltpu.CompilerParams(
            dimension_semantics=("arbitrary",), vmem_limit_bytes=VMEM_LIMIT),
        name="unsort",
    )(dst.reshape(n_tiles, rows), ys)


def kernel(x, norm_mix, w_in, pool_w, pool_scale, ssm_a_re, ssm_a_im, ssm_log_step, ssm_b_re, ssm_b_im, ssm_c_re, ssm_c_im, ssm_d, glu_w, glu_b, w_out, norm_ffn, router_coarse_w, router_coarse_b, router_fine_w, router_fine_b, exp_w_gate, exp_w_up, exp_w_down, norm_final):
    assert norm_mix.shape[0] == 1, "the experts call fuses the final norm: single-layer blocks only"
    b, l, d = x.shape
    wr, br = _router_weights(router_coarse_w[0], router_coarse_b[0], router_fine_w[0], router_fine_b[0])
    xs, dst, alloc, fin = _mixer(
        x, norm_mix[0], w_in[0], pool_w[0], pool_scale[0], ssm_a_re[0], ssm_a_im[0], ssm_log_step[0],
        ssm_b_re[0], ssm_b_im[0], ssm_c_re[0], ssm_c_im[0], ssm_d[0], glu_w[0], glu_b[0], w_out[0],
        norm_ffn[0], wr, br)
    ys = _experts(xs, alloc.reshape(-1), fin.reshape(-1), norm_ffn[0], wr, br,
                  exp_w_gate[0], exp_w_up[0], exp_w_down[0], norm_final)
    y = _unsort(ys, dst, b * l, d)
    return y.reshape(b, l, d)
```

```python
import functools

import jax
import jax.numpy as jnp
from jax import lax
from jax.experimental import pallas as pl
from jax.experimental.pallas import tpu as pltpu

EPS = 1e-6
POOL_WINDOWS = (2, 4, 8, 16)
POOL_GROUP = 128
POOL_TAIL = 16
POOL_HIST = 32
SSM_GROUP = 16
SSM_STATE = 64
N_EXPERT_GROUPS = 4
EXPERTS_PER_GROUP = 4
N_EXPERTS = 16
PAIRS_PER_GROUP = 6
N_CLASSES = N_EXPERT_GROUPS * PAIRS_PER_GROUP

SUBLANES = 8
LANES = 128
SLAB_GROUPS = 8
STATE_SLAB = SLAB_GROUPS * SSM_STATE
SCAN_LANES = 1024
TIME_TILE = 512
SCAN_TILE = 256
TILE_ROWS = 512
TILE_ROWS_LOG2 = 9
EXPERT_ROWS = 256
BF16_EXACT = 256
ROUTER_LANES = 128
VMEM_LIMIT = 56 * 1024 * 1024
NEG = -3.0e38

F32 = jnp.float32
BF16 = jnp.bfloat16
I32 = jnp.int32
NT_DIMS = (((1,), (1,)), ((), ()))


def _dot(a, b):
    return jnp.dot(a, b, preferred_element_type=F32)


def _dot_nt(a, b):
    return lax.dot_general(a, b, NT_DIMS, preferred_element_type=F32)


def _rms(x, g):
    return x * lax.rsqrt(jnp.mean(x * x, axis=-1, keepdims=True) + EPS) * g


def _rows8(*rows):
    r8 = lax.broadcasted_iota(I32, (SUBLANES, LANES), 0)
    out = jnp.zeros((SUBLANES, LANES), F32)
    for k, row in enumerate(rows):
        out = jnp.where(r8 == k, jnp.broadcast_to(row, (SUBLANES, LANES)), out)
    return out


def _route(logits, lane):
    is_c = lane < N_EXPERT_GROUPS
    cm = jnp.where(is_c, logits, NEG)
    cmax = jnp.max(cm, axis=-1, keepdims=True)
    gi = jnp.min(jnp.where(cm == cmax, lane, ROUTER_LANES), axis=-1, keepdims=True)
    lo = N_EXPERT_GROUPS + gi * EXPERTS_PER_GROUP
    insel = (lane >= lo) & (lane < lo + EXPERTS_PER_GROUP)
    fs = jnp.where(insel, logits, NEG)
    v0 = jnp.max(fs, axis=-1, keepdims=True)
    i0 = jnp.min(jnp.where(insel & (fs == v0), lane, ROUTER_LANES), axis=-1, keepdims=True)
    rest = insel & (lane != i0)
    fs2 = jnp.where(rest, logits, NEG)
    v1 = jnp.max(fs2, axis=-1, keepdims=True)
    i1 = jnp.min(jnp.where(rest & (fs2 == v1), lane, ROUTER_LANES), axis=-1, keepdims=True)
    return gi, lo, i0, i1


def _mixer_kernel(x_ref, nm_ref, win_ref, pw_ref, ps_ref, perm_ref, permt_ref, wb_ref,
                  are_ref, aim_ref, apre_ref, apim_ref, mkre_ref, mkim_ref, asre_ref, asim_ref,
                  wcre_ref, wcim_ref, d_ref, gw_ref, gb_ref, wout_ref, nf_ref, wr_ref, br_ref, tri_ref,
                  xs_hbm, dst_ref, alloc_ref, fin_ref,
                  zpad, lv2, lv4, lv8, bur, bui, st_r, st_i, x1t, meta_v, meta_s, alloc_v, ssem, msem, zsem,
                  *, tc, d_pool, d_ssm, n_tiles):
    bq = pl.program_id(0)
    n = pl.program_id(1)
    n_t = pl.num_programs(1)
    step = bq * n_t + n
    n_steps = pl.num_programs(0) * n_t
    slot = lax.rem(step, 2)
    seg = SCAN_TILE // SUBLANES
    n_sub = tc // SCAN_TILE
    n_slab = d_ssm // LANES
    n_chunk = n_slab * STATE_SLAB // SCAN_LANES
    per_iter = tc // (2 * n_chunk * seg * n_sub)
    assert per_iter * 2 * n_chunk * seg * n_sub == tc
    lane1 = lax.broadcasted_iota(I32, (1, LANES), 1)

    @pl.when(n == 0)
    def _():
        zpad[0:POOL_HIST, :] = jnp.zeros((POOL_HIST, d_pool), F32)
        st_r[...] = jnp.zeros_like(st_r)
        st_i[...] = jnp.zeros_like(st_i)

    @pl.when(step == 0)
    def _():
        lane_f = lane1.astype(F32)
        alloc_v[...] = _rows8(jnp.where(lane1 < N_CLASSES, lane_f, 0.0), jnp.zeros((1, LANES), F32),
                              jnp.full((1, LANES), float(N_CLASSES), F32))

    meta_cp = pltpu.make_async_copy(meta_v.at[0], meta_s.at[pl.ds(0, tc)], msem)

    def sent(s_):
        return pltpu.make_async_copy(x1t.at[s_], xs_hbm.at[pl.ds(0, tc * SUBLANES)], ssem.at[s_])

    def send_row(s_, r, prio):
        d = meta_s[r]
        pltpu.make_async_copy(x1t.at[s_, pl.ds(pl.multiple_of(r * SUBLANES, SUBLANES), SUBLANES)],
                              xs_hbm.at[pl.ds(pl.multiple_of(d * SUBLANES, SUBLANES), SUBLANES)],
                              ssem.at[s_]).start(priority=prio)

    def send_rows(s_):
        def send(r2, carry):
            for prio in range(2):
                send_row(s_, r2 * 2 + prio, prio)
            return carry

        lax.fori_loop(0, tc // 2, send, 0, unroll=4)

    def send_some(r_first):
        for k in range(per_iter):
            send_row(1 - slot, r_first + k, k % 2)

    x = x_ref[0]
    hn = _rms(x, nm_ref[...]).astype(BF16)
    z = _dot(hn, win_ref[...])

    zp = z[:, :d_pool]
    top = POOL_HIST + tc
    zpad[POOL_HIST:top, :] = zp
    g1, g2, g3 = POOL_GROUP, 2 * POOL_GROUP, 3 * POOL_GROUP
    lv2[8:top, :] = zpad[8:top, :] + zpad[7:top - 1, :]
    lv4[16:top, :] = lv2[16:top, g1:] + lv2[14:top - 2, g1:]
    lv8[24:top, :] = lv4[24:top, g1:] + lv4[20:top - 4, g1:]
    wins = (lv2[POOL_HIST:top, 0:g1], lv4[POOL_HIST:top, 0:g1], lv8[POOL_HIST:top, 0:g1],
            lv8[POOL_HIST:top, g1:] + lv8[POOL_HIST - 8:top - 8, g1:])
    t = n * tc + lax.broadcasted_iota(I32, (tc, 1), 0)
    pooled = []
    for gi, w in enumerate(POOL_WINDOWS):
        cols = slice(gi * POOL_GROUP, (gi + 1) * POOL_GROUP)
        cnt = jnp.minimum(t + 1, w).astype(F32)
        p = wins[gi] / cnt - zp[:, cols]
        pooled.append(_dot(p.astype(BF16), pw_ref[gi]))
    y_pool = jnp.concatenate(pooled, axis=-1) * ps_ref[...]
    zpad[POOL_HIST - POOL_TAIL:POOL_HIST, :] = zpad[top - POOL_TAIL:top, :]

    u = z[:, d_pool:]
    ub = u.astype(BF16)
    up = jnp.concatenate([_dot(perm_ref[...], ub[h * SCAN_TILE:(h + 1) * SCAN_TILE]) for h in range(n_sub)],
                         axis=0)
    upb = up.astype(BF16)
    for m in range(n_slab):
        r = _dot(upb[:, m * LANES:(m + 1) * LANES], wb_ref[m])
        bur[:, m * STATE_SLAB:(m + 1) * STATE_SLAB] = r[:, :STATE_SLAB]
        bui[:, m * STATE_SLAB:(m + 1) * STATE_SLAB] = r[:, STATE_SLAB:]

    def scan_all(sending):
        row = lax.broadcasted_iota(I32, (SUBLANES, SCAN_LANES), 0)
        for h in range(n_sub):
            for c in range(n_chunk):
                scan_chunk(h, c, row, sending)

    def scan_chunk(h, c, row, sending):
        cs = slice(c * SCAN_LANES, (c + 1) * SCAN_LANES)
        row0 = h * SCAN_TILE
        loop0 = (h * n_chunk + c) * 2 * seg
        ar = are_ref[:, cs]
        ai = aim_ref[:, cs]

        def scan_body(s, carry):
            xr, xi = carry
            r0 = pl.multiple_of(row0 + s * SUBLANES, SUBLANES)
            nxr = ar * xr - ai * xi + bur[pl.ds(r0, SUBLANES), cs]
            nxi = ar * xi + ai * xr + bui[pl.ds(r0, SUBLANES), cs]
            bur[pl.ds(r0, SUBLANES), cs] = nxr
            bui[pl.ds(r0, SUBLANES), cs] = nxi
            if sending:
                send_some((loop0 + s) * per_iter)
            return nxr, nxi

        zero = jnp.zeros((SUBLANES, SCAN_LANES), F32)
        er, ei = lax.fori_loop(0, seg, scan_body, (zero, zero), unroll=4)

        fr = jnp.where(row == 0, st_r[:, cs], pltpu.roll(er, 1, 0))
        fi = jnp.where(row == 0, st_i[:, cs], pltpu.roll(ei, 1, 0))
        for ki, k in enumerate((1, 2, 4)):
            mr = mkre_ref[ki, :, cs]
            mi = mkim_ref[ki, :, cs]
            rr = pltpu.roll(fr, k, 0)
            ri = pltpu.roll(fi, k, 0)
            fr, fi = fr + mr * rr - mi * ri, fi + mr * ri + mi * rr
        asr = asre_ref[:, cs]
        asi = asim_ref[:, cs]
        nsr = asr * fr - asi * fi + er
        nsi = asr * fi + asi * fr + ei
        st_r[:, cs] = pltpu.roll(nsr, 1, 0)
        st_i[:, cs] = pltpu.roll(nsi, 1, 0)

        def fix_body(s, carry):
            t0 = pl.multiple_of(s * SUBLANES, SUBLANES)
            r0 = pl.multiple_of(row0 + s * SUBLANES, SUBLANES)
            pr = apre_ref[pl.ds(t0, SUBLANES), cs]
            pi = apim_ref[pl.ds(t0, SUBLANES), cs]
            bur[pl.ds(r0, SUBLANES), cs] = bur[pl.ds(r0, SUBLANES), cs] + pr * fr - pi * fi
            bui[pl.ds(r0, SUBLANES), cs] = bui[pl.ds(r0, SUBLANES), cs] + pr * fi + pi * fr
            if sending:
                send_some((loop0 + seg + s) * per_iter)
            return carry

        lax.fori_loop(0, seg, fix_body, 0, unroll=4)

    @pl.when(step == 0)
    def _():
        scan_all(False)

    @pl.when(step >= 1)
    def _():
        meta_cp.wait()
        scan_all(True)

    ys = []
    for m in range(n_slab):
        cs = slice(m * STATE_SLAB, (m + 1) * STATE_SLAB)
        ys.append(_dot(bur[:, cs].astype(BF16), wcre_ref[m]) + _dot(bui[:, cs].astype(BF16), wcim_ref[m]))
    y = jnp.concatenate(ys, axis=-1) + d_ref[...] * up
    y = jax.nn.gelu(y)
    y = y * jax.nn.sigmoid(_dot(y.astype(BF16), gw_ref[...]) + gb_ref[...])
    yb = y.astype(BF16)
    y_ssm = jnp.concatenate([_dot(permt_ref[...], yb[h * SCAN_TILE:(h + 1) * SCAN_TILE]) for h in range(n_sub)],
                            axis=0)

    mix = _dot(y_pool.astype(BF16), wout_ref[0:d_pool, :]) + _dot(y_ssm.astype(BF16), wout_ref[d_pool:, :])
    x1 = x + mix

    hb = _rms(x1, nf_ref[...]).astype(BF16)
    logits = _dot(hb, wr_ref[...]) + br_ref[...]
    lane = lax.broadcasted_iota(I32, (tc, ROUTER_LANES), 1)
    gi, lo, i0, i1 = _route(logits, lane)
    ea = jnp.minimum(i0, i1) - lo
    eb = jnp.maximum(i0, i1) - lo
    cls = gi * PAIRS_PER_GROUP + ((ea * (7 - ea)) >> 1) + (eb - ea - 1)
    oh = (lane == cls).astype(F32)
    ohb = oh.astype(BF16)
    before = _dot(tri_ref[...], ohb) * oh
    ones8 = jnp.ones((SUBLANES, ROUTER_LANES), BF16)
    before_hi = jnp.floor(before * (1.0 / BF16_EXACT))
    before_lo = before - before_hi * float(BF16_EXACT)
    rank = (_dot_nt(ones8, before_hi.astype(BF16))[0:1, :] * float(BF16_EXACT)
            + _dot_nt(ones8, before_lo.astype(BF16))[0:1, :])
    cnt = jnp.sum(oh, axis=0, keepdims=True)

    st = alloc_v[...]
    open_t, fill, free = st[0:1], st[1:2], st[2:3]
    room = float(TILE_ROWS) - fill
    spill = cnt > room
    spill_f = spill.astype(F32)
    upper = (lax.broadcasted_iota(I32, (LANES, LANES), 0) < lax.broadcasted_iota(I32, (LANES, LANES), 1))
    earlier = _dot(_rows8(spill_f).astype(BF16), upper.astype(F32).astype(BF16))[0:1]
    new_t = free + earlier
    fill_hi = jnp.floor(fill * (1.0 / BF16_EXACT))
    per_row = _dot_nt(_rows8(open_t, new_t, fill_hi, fill - fill_hi * float(BF16_EXACT)).astype(BF16), ohb)
    open_r, new_r = per_row[0:1], per_row[1:2]
    fill_r = per_row[2:3] * float(BF16_EXACT) + per_row[3:4]
    room_r = float(TILE_ROWS) - fill_r
    dst = jnp.where(rank < room_r, open_r * float(TILE_ROWS) + fill_r + rank,
                    new_r * float(TILE_ROWS) + rank - room_r).astype(I32)
    dst_ref[0] = dst
    alloc_ref[0] = jnp.where(spill, new_t, -1.0).astype(I32)
    new_state = _rows8(jnp.where(spill, new_t, open_t), jnp.where(spill, cnt - room, fill + cnt),
                       free + jnp.sum(spill_f, axis=-1, keepdims=True))
    alloc_v[...] = new_state
    fin_ref[...] = new_state.astype(I32)

    @pl.when(step >= 2)
    def _():
        sent(slot).wait()

    for c in range(SUBLANES):
        x1t[slot, pl.ds(c, tc, stride=SUBLANES), :] = x1[:, c * LANES:(c + 1) * LANES]
    meta_v[0:1, :] = dst
    meta_cp.start()

    @pl.when(step == n_steps - 1)
    def _():
        meta_cp.wait()
        send_rows(slot)

        @pl.when(step >= 1)
        def _():
            sent(1 - slot).wait()


        zsrc = x1t.at[1 - slot]
        zsrc[...] = jnp.zeros((tc * SUBLANES, LANES), F32)
        meta_v[:, 0:LANES] = new_state.astype(I32)
        for k in range(3):
            cp = pltpu.make_async_copy(meta_v.at[k, pl.ds(0, LANES)], meta_s.at[pl.ds(k * LANES, LANES)], msem)
            cp.start()
            cp.wait()

        def zcopy(first_row, n_rows):
            return pltpu.make_async_copy(
                zsrc.at[pl.ds(0, n_rows * SUBLANES)],
                xs_hbm.at[pl.ds(pl.multiple_of(first_row * SUBLANES, SUBLANES), n_rows * SUBLANES)], zsem)

        for c in range(N_CLASSES):
            tile = meta_s[c]
            fill_c = meta_s[LANES + c]
            rem = TILE_ROWS - fill_c
            for bit in range(TILE_ROWS_LOG2, -1, -1):
                size = 1 << bit
                first = (tile << TILE_ROWS_LOG2) + fill_c + (rem & ~(2 * size - 1))

                @pl.when((rem & size) != 0)
                def _(first=first, size=size):
                    cp = zcopy(first, size)
                    cp.start()
                    cp.wait()

        def tail(k, carry):
            cp = zcopy(k << TILE_ROWS_LOG2, TILE_ROWS)
            cp.start()
            cp.wait()
            return carry

        lax.fori_loop(meta_s[2 * LANES], n_tiles, tail, 0)
        sent(slot).wait()


def _ssm_tables(a_re, a_im, log_step, b_re, b_im, c_re, c_im, seg):
    g, p = a_re.shape
    h = b_re.shape[-1]
    n_slab = g // SLAB_GROUPS
    lr, li = a_re.astype(F32), a_im.astype(F32)
    step = jnp.exp(log_step.astype(F32))[:, None]
    mag = jnp.exp(lr * step)
    ab_re = mag * jnp.cos(li * step)
    ab_im = mag * jnp.sin(li * step)
    den = lr * lr + li * li
    nr, ni = ab_re - 1.0, ab_im
    q_re = (nr * lr + ni * li) / den
    q_im = (ni * lr - nr * li) / den
    br, bi = b_re.astype(F32), b_im.astype(F32)
    bb_re = q_re[..., None] * br - q_im[..., None] * bi
    bb_im = q_re[..., None] * bi + q_im[..., None] * br

    eye = jnp.eye(SLAB_GROUPS, dtype=F32)

    def b_slab(bb):
        bb = bb.reshape(n_slab, SLAB_GROUPS, p, h)
        w = jnp.einsum('mgph,gk->mghkp', bb, eye)
        return w.reshape(n_slab, SLAB_GROUPS * h, SLAB_GROUPS * p)

    wb = jnp.concatenate([b_slab(bb_re), b_slab(bb_im)], axis=-1).astype(BF16)

    def c_slab(cc):
        cc = cc.astype(F32).reshape(n_slab, SLAB_GROUPS, h, p)
        w = jnp.einsum('mghp,gk->mgpkh', cc, eye)
        return w.reshape(n_slab, SLAB_GROUPS * p, SLAB_GROUPS * h)

    wc_re = c_slab(c_re).astype(BF16)
    wc_im = c_slab(-c_im).astype(BF16)

    def apow(nn):
        nn = nn.astype(F32)[:, None, None]
        m_ = jnp.exp(nn * (lr * step))
        ang = nn * (li * step)
        return (m_ * jnp.cos(ang)).reshape(-1, g * p), (m_ * jnp.sin(ang)).reshape(-1, g * p)

    a1r, a1i = apow(jnp.ones((1,), F32))
    a_r = jnp.broadcast_to(a1r, (SUBLANES, g * p))
    a_i = jnp.broadcast_to(a1i, (SUBLANES, g * p))
    pr, pi = apow(jnp.arange(1, seg + 1))
    ap_r = jnp.repeat(pr, SUBLANES, axis=0)
    ap_i = jnp.repeat(pi, SUBLANES, axis=0)
    ks = jnp.array([1, 2, 4])
    mr, mi = apow(ks * seg)
    rows = jnp.arange(SUBLANES)[None, :, None]
    mask = (rows >= ks[:, None, None]).astype(F32)
    mk_r = mr[:, None, :] * mask
    mk_i = mi[:, None, :] * mask
    sr, si = apow(jnp.full((1,), seg, F32))
    as_r = jnp.broadcast_to(sr, (SUBLANES, g * p))
    as_i = jnp.broadcast_to(si, (SUBLANES, g * p))
    return wb, a_r, a_i, ap_r, ap_i, mk_r, mk_i, as_r, as_i, wc_re, wc_im


def _router_weights(w_coarse, b_coarse, w_fine, b_fine):
    d = w_coarse.shape[0]
    pad = ROUTER_LANES - N_EXPERT_GROUPS - N_EXPERTS
    wr = jnp.concatenate([w_coarse, w_fine, jnp.zeros((d, pad), F32)], axis=-1).astype(BF16)
    br = jnp.concatenate([b_coarse, b_fine, jnp.zeros((pad,), F32)]).reshape(1, -1).astype(F32)
    return wr, br


def _mixer(x, norm_mix, w_in, pool_w, pool_scale, a_re, a_im, log_step, b_re, b_im, c_re, c_im,
           d_skip, glu_w, glu_b, w_out, norm_ffn, wr, br):
    b, l, d = x.shape
    d_pool = pool_scale.shape[-1]
    d_ssm = d_skip.shape[-1]
    tc = TIME_TILE
    assert l % tc == 0 and tc % SCAN_TILE == 0
    assert d_pool == len(POOL_WINDOWS) * POOL_GROUP and d_ssm % LANES == 0
    assert d == SUBLANES * LANES, "one token row must fill exactly one (8, 128) tile"
    assert tc == TILE_ROWS, "a mixer tile must never need more than one new sorted tile per class"
    assert N_CLASSES <= LANES
    seg = SCAN_TILE // SUBLANES
    n_state = a_re.shape[0] * a_re.shape[1]
    assert (b * l) // TILE_ROWS + N_CLASSES <= BF16_EXACT, "tile indices pass through bf16 operands"
    tables = _ssm_tables(a_re, a_im, log_step, b_re, b_im, c_re, c_im, seg)
    wb, a_r, a_i, ap_r, ap_i, mk_r, mk_i, as_r, as_i, wc_re, wc_im = tables

    i = jnp.arange(SCAN_TILE)
    src = (i % SUBLANES) * seg + i // SUBLANES
    perm = (src[:, None] == i[None, :]).astype(BF16)
    permt = perm.T
    it = jnp.arange(tc)
    tri = (it[None, :] < it[:, None]).astype(BF16)

    row2 = lambda a: a.reshape(1, -1).astype(F32)
    operands = [
        x, row2(norm_mix), w_in.astype(BF16), pool_w.astype(BF16), row2(pool_scale), perm, permt, wb,
        a_r, a_i, ap_r, ap_i, mk_r, mk_i, as_r, as_i, wc_re, wc_im,
        row2(d_skip), glu_w.astype(BF16), row2(glu_b), w_out.astype(BF16),
        row2(norm_ffn), wr, br, tri,
    ]

    def whole(a):
        nd = a.ndim
        return pl.BlockSpec(a.shape, lambda bi, ni, nd=nd: (0,) * nd)

    n_t = l // tc
    n_tiles = (b * l) // TILE_ROWS + N_CLASSES
    in_specs = [pl.BlockSpec((1, tc, d), lambda bi, ni: (bi, ni, 0))] + [whole(a) for a in operands[1:]]
    kern = functools.partial(_mixer_kernel, tc=tc, d_pool=d_pool, d_ssm=d_ssm, n_tiles=n_tiles)
    return pl.pallas_call(
        kern,
        grid=(b, n_t),
        in_specs=in_specs,
        out_specs=[
            pl.BlockSpec(memory_space=pl.ANY),
            pl.BlockSpec((1, 1, tc), lambda bi, ni: (bi * n_t + ni, 0, 0)),
            pl.BlockSpec((1, 1, LANES), lambda bi, ni: (bi * n_t + ni, 0, 0)),
            pl.BlockSpec((SUBLANES, LANES), lambda bi, ni: (0, 0)),
        ],
        out_shape=[
            jax.ShapeDtypeStruct((n_tiles * TILE_ROWS * SUBLANES, LANES), F32),
            jax.ShapeDtypeStruct((b * n_t, 1, tc), I32),
            jax.ShapeDtypeStruct((b * n_t, 1, LANES), I32),
            jax.ShapeDtypeStruct((SUBLANES, LANES), I32),
        ],
        scratch_shapes=[
            pltpu.VMEM((POOL_HIST + tc, d_pool), F32),
            pltpu.VMEM((POOL_HIST + tc, d_pool), F32),
            pltpu.VMEM((POOL_HIST + tc, d_pool - POOL_GROUP), F32),
            pltpu.VMEM((POOL_HIST + tc, d_pool - 2 * POOL_GROUP), F32),
            pltpu.VMEM((tc, n_state), F32),
            pltpu.VMEM((tc, n_state), F32),
            pltpu.VMEM((SUBLANES, n_state), F32),
            pltpu.VMEM((SUBLANES, n_state), F32),
            pltpu.VMEM((2, tc * SUBLANES, LANES), F32),
            pltpu.VMEM((SUBLANES, tc), I32),
            pltpu.SMEM((max(tc, 3 * LANES),), I32),
            pltpu.VMEM((SUBLANES, LANES), F32),
            pltpu.SemaphoreType.DMA((2,)),
            pltpu.SemaphoreType.DMA,
            pltpu.SemaphoreType.DMA,
        ],
        compiler_params=pltpu.CompilerParams(
            dimension_semantics=("arbitrary", "arbitrary"), vmem_limit_bytes=VMEM_LIMIT),
        name="mixer",
    )(*operands)


def _experts_kernel(alloc_ref, fin_ref, x_ref, nf_ref, wr_ref, br_ref, wg_hbm, wu_hbm, wd_hbm,
                    nfin_ref, o_ref, wg_ref, wu_ref, wd_ref, tcls, tnv, wsem, *, n_steps):
    i = pl.program_id(0)
    n_tiles = pl.num_programs(0)
    rows = EXPERT_ROWS

    @pl.when(i == 0)
    def _():
        copies = [pltpu.make_async_copy(src, dst, wsem.at[j])
                  for j, (src, dst) in enumerate(((wg_hbm, wg_ref), (wu_hbm, wu_ref), (wd_hbm, wd_ref)))]
        for cp in copies:
            cp.start()
        free = fin_ref[2 * LANES]

        def init(k, carry):
            tcls[k] = jnp.minimum(k, N_CLASSES - 1)
            tnv[k] = jnp.where(k < free, TILE_ROWS, 0)
            return carry

        lax.fori_loop(0, n_tiles, init, 0)

        def log(s, carry):
            for c in range(N_CLASSES):
                t = alloc_ref[s * LANES + c]
                tcls[jnp.where(t >= 0, t, n_tiles)] = c
            return carry

        lax.fori_loop(0, n_steps, log, 0)
        for c in range(N_CLASSES):
            tnv[fin_ref[c]] = fin_ref[LANES + c]
        for cp in copies:
            cp.wait()

    c = tcls[i]
    g = (c >= PAIRS_PER_GROUP).astype(I32) + (c >= 2 * PAIRS_PER_GROUP).astype(I32) \
        + (c >= 3 * PAIRS_PER_GROUP).astype(I32)
    p = c - g * PAIRS_PER_GROUP
    a = (p >= 3).astype(I32) + (p >= 5).astype(I32)
    b = p - ((a * (7 - a)) >> 1) + a + 1
    ea = g * EXPERTS_PER_GROUP + a
    eb = g * EXPERTS_PER_GROUP + b

    def part(h):
        first = h * rows * SUBLANES

        @pl.when(tnv[i] <= h * rows)
        def _():
            o_ref[pl.ds(first, rows * SUBLANES), :] = jnp.zeros((rows * SUBLANES, LANES), F32)

        @pl.when(tnv[i] > h * rows)
        def _():
            compute(first)

    def compute(first):
        x = jnp.concatenate([x_ref[pl.ds(first + k, rows, stride=SUBLANES), :] for k in range(SUBLANES)], axis=-1)
        hb = _rms(x, nf_ref[...]).astype(BF16)
        logits = _dot(hb, wr_ref[...]) + br_ref[...]
        lane = lax.broadcasted_iota(I32, (rows, ROUTER_LANES), 1)
        is_c = lane < N_EXPERT_GROUPS
        cmax = jnp.max(jnp.where(is_c, logits, NEG), axis=-1, keepdims=True)
        psum = jnp.sum(jnp.where(is_c, jnp.exp(logits - cmax), 0.0), axis=-1, keepdims=True)
        cg = jnp.sum(jnp.where(lane == g, logits, 0.0), axis=-1, keepdims=True)
        p_g = jnp.exp(cg - cmax) / psum
        va = jnp.sum(jnp.where(lane == N_EXPERT_GROUPS + ea, logits, 0.0), axis=-1, keepdims=True)
        vb = jnp.sum(jnp.where(lane == N_EXPERT_GROUPS + eb, logits, 0.0), axis=-1, keepdims=True)
        vmax = jnp.maximum(va, vb)
        exa = jnp.exp(va - vmax)
        exb = jnp.exp(vb - vmax)
        den = exa + exb
        y = jnp.zeros_like(x)
        for e, w in ((ea, p_g * (exa / den)), (eb, p_g * (exb / den))):
            act = jax.nn.silu(_dot(hb, wg_ref[e])) * _dot(hb, wu_ref[e])
            y = y + _dot((act * w).astype(BF16), wd_ref[e])
        out = _rms(x + y, nfin_ref[...])
        for k in range(SUBLANES):
            o_ref[pl.ds(first + k, rows, stride=SUBLANES), :] = out[:, k * LANES:(k + 1) * LANES]

    for h in range(TILE_ROWS // rows):
        part(h)


def _experts(xs, alloc, fin, norm_ffn, wr, br, w_gate, w_up, w_down, norm_final):
    n_e, d, f = w_gate.shape
    assert n_e == N_EXPERTS
    blk = TILE_ROWS * SUBLANES
    n_tiles = xs.shape[0] // blk
    n_steps = alloc.shape[0] // LANES
    row2 = lambda a: a.reshape(1, -1).astype(F32)
    const = lambda shape: pl.BlockSpec(shape, lambda i, al, fi: (0,) * len(shape))
    anyspec = pl.BlockSpec(memory_space=pl.ANY)
    grid_spec = pltpu.PrefetchScalarGridSpec(
        num_scalar_prefetch=2,
        grid=(n_tiles,),
        in_specs=[
            pl.BlockSpec((blk, LANES), lambda i, al, fi: (i, 0)),
            const((1, d)), const((d, ROUTER_LANES)), const((1, ROUTER_LANES)),
            anyspec, anyspec, anyspec,
            const((1, d)),
        ],
        out_specs=pl.BlockSpec((blk, LANES), lambda i, al, fi: (i, 0)),
        scratch_shapes=[
            pltpu.VMEM((n_e, d, f), BF16),
            pltpu.VMEM((n_e, d, f), BF16),
            pltpu.VMEM((n_e, f, d), BF16),
            pltpu.SMEM((n_tiles + 1,), I32),
            pltpu.SMEM((n_tiles + 1,), I32),
            pltpu.SemaphoreType.DMA((3,)),
        ],
    )
    return pl.pallas_call(
        functools.partial(_experts_kernel, n_steps=n_steps),
        grid_spec=grid_spec,
        out_shape=jax.ShapeDtypeStruct(xs.shape, F32),
        compiler_params=pltpu.CompilerParams(
            dimension_semantics=("arbitrary",), vmem_limit_bytes=VMEM_LIMIT),
        name="experts",
    )(alloc, fin, xs, row2(norm_ffn), wr, br,
      w_gate.astype(BF16), w_up.astype(BF16), w_down.astype(BF16), row2(norm_final))


def _unsort_kernel(dst_hbm, ys_hbm, o_ref, idx, buf, isem, gsem):
    i = pl.program_id(0)
    nt = pl.num_programs(0)
    slot = lax.rem(i, 2)
    rows = o_ref.shape[0]

    def idx_copy(tile, s3):
        return pltpu.make_async_copy(dst_hbm.at[tile], idx.at[pl.ds(s3 * rows, rows)], isem.at[s3])

    def gather_start(s3, bslot):
        def body(r2, c):
            for prio in range(2):
                r = r2 * 2 + prio
                src = idx[s3 * rows + r]
                pltpu.make_async_copy(ys_hbm.at[pl.ds(pl.multiple_of(src * SUBLANES, SUBLANES), SUBLANES)],
                                      buf.at[bslot, pl.ds(pl.multiple_of(r * SUBLANES, SUBLANES), SUBLANES)],
                                      gsem.at[bslot]).start(priority=prio)
            return c

        lax.fori_loop(0, rows // 2, body, 0, unroll=4)

    @pl.when(i == 0)
    def _():
        idx_copy(0, 0).start()
        idx_copy(0, 0).wait()

        @pl.when(nt > 1)
        def _():
            idx_copy(1, 1).start()
            idx_copy(1, 1).wait()

        gather_start(0, 0)

    @pl.when(i + 2 < nt)
    def _():
        idx_copy(i + 2, lax.rem(i + 2, 3)).start()

    @pl.when((i >= 1) & (i + 1 < nt))
    def _():
        idx_copy(i + 1, lax.rem(i + 1, 3)).wait()

    @pl.when(i + 1 < nt)
    def _():
        gather_start(lax.rem(i + 1, 3), 1 - slot)

    pltpu.make_async_copy(ys_hbm.at[pl.ds(0, rows * SUBLANES)], buf.at[slot], gsem.at[slot]).wait()
    o_ref[...] = jnp.concatenate(
        [buf[slot, pl.ds(c, rows, stride=SUBLANES), :] for c in range(SUBLANES)], axis=-1)


def _unsort(ys, dst, n_tok, d):
    n_tiles, _, rows = dst.shape
    anyspec = pl.BlockSpec(memory_space=pl.ANY)
    return pl.pallas_call(
        _unsort_kernel,
        grid=(n_tiles,),
        in_specs=[anyspec, anyspec],
        out_specs=pl.BlockSpec((rows, d), lambda i: (i, 0)),
        out_shape=jax.ShapeDtypeStruct((n_tok, d), F32),
        scratch_shapes=[
            pltpu.SMEM((3 * rows,), I32),
            pltpu.VMEM((2, rows * SUBLANES, LANES), F32),
            pltpu.SemaphoreType.DMA((3,)),
            pltpu.SemaphoreType.DMA((2,)),
        ],
        compiler_params=pltpu.CompilerParams(
            dimension_semantics=("arbitrary",), vmem_limit_bytes=VMEM_LIMIT),
        name="unsort",
    )(dst.reshape(n_tiles, rows), ys)


def kernel(x, norm_mix, w_in, pool_w, pool_scale, ssm_a_re, ssm_a_im, ssm_log_step, ssm_b_re, ssm_b_im, ssm_c_re, ssm_c_im, ssm_d, glu_w, glu_b, w_out, norm_ffn, router_coarse_w, router_coarse_b, router_fine_w, router_fine_b, exp_w_gate, exp_w_up, exp_w_down, norm_final):
    assert norm_mix.shape[0] == 1, "the experts call fuses the final norm: single-layer blocks only"
    b, l, d = x.shape
    wr, br = _router_weights(router_coarse_w[0], router_coarse_b[0], router_fine_w[0], router_fine_b[0])
    xs, dst, alloc, fin = _mixer(
        x, norm_mix[0], w_in[0], pool_w[0], pool_scale[0], ssm_a_re[0], ssm_a_im[0], ssm_log_step[0],
        ssm_b_re[0], ssm_b_im[0], ssm_c_re[0], ssm_c_im[0], ssm_d[0], glu_w[0], glu_b[0], w_out[0],
        norm_ffn[0], wr, br)
    ys = _experts(xs, alloc.reshape(-1), fin.reshape(-1), norm_ffn[0], wr, br,
                  exp_w_gate[0], exp_w_up[0], exp_w_down[0], norm_final)
    y = _unsort(ys, dst, b * l, d)
    return y.reshape(b, l, d)
```

```python
import functools

import jax
import jax.numpy as jnp
from jax import lax
from jax.experimental import pallas as pl
from jax.experimental.pallas import tpu as pltpu

EPS = 1e-6
POOL_WINDOWS = (2, 4, 8, 16)
POOL_GROUP = 128
POOL_TAIL = 16
POOL_HIST = 32
SSM_GROUP = 16
SSM_STATE = 64
N_EXPERT_GROUPS = 4
EXPERTS_PER_GROUP = 4
N_EXPERTS = 16
PAIRS_PER_GROUP = 6
N_CLASSES = N_EXPERT_GROUPS * PAIRS_PER_GROUP

SUBLANES = 8
LANES = 128
SLAB_GROUPS = 8
STATE_SLAB = SLAB_GROUPS * SSM_STATE
SCAN_LANES = 1024
TIME_TILE = 512
SCAN_TILE = 256
TILE_ROWS = 512
TILE_ROWS_LOG2 = 9
EXPERT_ROWS = 256
BF16_EXACT = 256
ROUTER_LANES = 128
VMEM_LIMIT = 56 * 1024 * 1024
NEG = -3.0e38

F32 = jnp.float32
BF16 = jnp.bfloat16
I32 = jnp.int32
NT_DIMS = (((1,), (1,)), ((), ()))


def _dot(a, b):
    return jnp.dot(a, b, preferred_element_type=F32)


def _dot_nt(a, b):
    return lax.dot_general(a, b, NT_DIMS, preferred_element_type=F32)


def _rms(x, g):
    return x * lax.rsqrt(jnp.mean(x * x, axis=-1, keepdims=True) + EPS) * g


def _rows8(*rows):
    r8 = lax.broadcasted_iota(I32, (SUBLANES, LANES), 0)
    out = jnp.zeros((SUBLANES, LANES), F32)
    for k, row in enumerate(rows):
        out = jnp.where(r8 == k, jnp.broadcast_to(row, (SUBLANES, LANES)), out)
    return out


def _route(logits, lane):
    is_c = lane < N_EXPERT_GROUPS
    cm = jnp.where(is_c, logits, NEG)
    cmax = jnp.max(cm, axis=-1, keepdims=True)
    gi = jnp.min(jnp.where(cm == cmax, lane, ROUTER_LANES), axis=-1, keepdims=True)
    lo = N_EXPERT_GROUPS + gi * EXPERTS_PER_GROUP
    insel = (lane >= lo) & (lane < lo + EXPERTS_PER_GROUP)
    fs = jnp.where(insel, logits, NEG)
    v0 = jnp.max(fs, axis=-1, keepdims=True)
    i0 = jnp.min(jnp.where(insel & (fs == v0), lane, ROUTER_LANES), axis=-1, keepdims=True)
    rest = insel & (lane != i0)
    fs2 = jnp.where(rest, logits, NEG)
    v1 = jnp.max(fs2, axis=-1, keepdims=True)
    i1 = jnp.min(jnp.where(rest & (fs2 == v1), lane, ROUTER_LANES), axis=-1, keepdims=True)
    return gi, lo, i0, i1


def _mixer_kernel(x_ref, nm_ref, win_ref, pw_ref, ps_ref, perm_ref, permt_ref, wb_ref,
                  are_ref, aim_ref, apre_ref, apim_ref, mkre_ref, mkim_ref, asre_ref, asim_ref,
                  wcre_ref, wcim_ref, d_ref, gw_ref, gb_ref, wout_ref, nf_ref, wr_ref, br_ref, tri_ref,
                  xs_hbm, dst_ref, alloc_ref, fin_ref,
                  zpad, lv2, lv4, lv8, bur, bui, st_r, st_i, x1t, meta_v, meta_s, alloc_v, ssem, msem, zsem,
                  *, tc, d_pool, d_ssm, n_tiles):
    bq = pl.program_id(0)
    n = pl.program_id(1)
    n_t = pl.num_programs(1)
    step = bq * n_t + n
    n_steps = pl.num_programs(0) * n_t
    slot = lax.rem(step, 2)
    seg = SCAN_TILE // SUBLANES
    n_sub = tc // SCAN_TILE
    n_slab = d_ssm // LANES
    n_chunk = n_slab * STATE_SLAB // SCAN_LANES
    per_iter = tc // (2 * n_chunk * seg * n_sub)
    assert per_iter * 2 * n_chunk * seg * n_sub == tc
    lane1 = lax.broadcasted_iota(I32, (1, LANES), 1)

    @pl.when(n == 0)
    def _():
        zpad[0:POOL_HIST, :] = jnp.zeros((POOL_HIST, d_pool), F32)
        st_r[...] = jnp.zeros_like(st_r)
        st_i[...] = jnp.zeros_like(st_i)

    @pl.when(step == 0)
    def _():
        lane_f = lane1.astype(F32)
        alloc_v[...] = _rows8(jnp.where(lane1 < N_CLASSES, lane_f, 0.0), jnp.zeros((1, LANES), F32),
                              jnp.full((1, LANES), float(N_CLASSES), F32))
        x1t[1 - slot] = jnp.zeros((tc * SUBLANES, LANES), F32)

        def junk(r, carry):
            meta_s[r] = (n_tiles - 1) * TILE_ROWS + r
            return carry

        lax.fori_loop(0, tc, junk, 0)

    meta_cp = pltpu.make_async_copy(meta_v.at[0], meta_s.at[pl.ds(0, tc)], msem)

    @pl.when(step >= 1)
    def _():
        meta_cp.wait()

    def sent(s_):
        return pltpu.make_async_copy(x1t.at[s_], xs_hbm.at[pl.ds(0, tc * SUBLANES)], ssem.at[s_])

    def send_row(s_, r, prio):
        d = meta_s[r]
        pltpu.make_async_copy(x1t.at[s_, pl.ds(pl.multiple_of(r * SUBLANES, SUBLANES), SUBLANES)],
                              xs_hbm.at[pl.ds(pl.multiple_of(d * SUBLANES, SUBLANES), SUBLANES)],
                              ssem.at[s_]).start(priority=prio)

    def send_rows(s_):
        def send(r2, carry):
            for prio in range(2):
                send_row(s_, r2 * 2 + prio, prio)
            return carry

        lax.fori_loop(0, tc // 2, send, 0, unroll=4)

    def send_some(r_first):
        for k in range(per_iter):
            send_row(1 - slot, r_first + k, k % 2)

    x = x_ref[0]
    hn = _rms(x, nm_ref[...]).astype(BF16)
    z = _dot(hn, win_ref[...])

    zp = z[:, :d_pool]
    top = POOL_HIST + tc
    zpad[POOL_HIST:top, :] = zp
    g1, g2, g3 = POOL_GROUP, 2 * POOL_GROUP, 3 * POOL_GROUP
    lv2[8:top, :] = zpad[8:top, :] + zpad[7:top - 1, :]
    lv4[16:top, :] = lv2[16:top, g1:] + lv2[14:top - 2, g1:]
    lv8[24:top, :] = lv4[24:top, g1:] + lv4[20:top - 4, g1:]
    wins = (lv2[POOL_HIST:top, 0:g1], lv4[POOL_HIST:top, 0:g1], lv8[POOL_HIST:top, 0:g1],
            lv8[POOL_HIST:top, g1:] + lv8[POOL_HIST - 8:top - 8, g1:])
    t = n * tc + lax.broadcasted_iota(I32, (tc, 1), 0)
    pooled = []
    for gi, w in enumerate(POOL_WINDOWS):
        cols = slice(gi * POOL_GROUP, (gi + 1) * POOL_GROUP)
        cnt = jnp.minimum(t + 1, w).astype(F32)
        p = wins[gi] / cnt - zp[:, cols]
        pooled.append(_dot(p.astype(BF16), pw_ref[gi]))
    y_pool = jnp.concatenate(pooled, axis=-1) * ps_ref[...]
    zpad[POOL_HIST - POOL_TAIL:POOL_HIST, :] = zpad[top - POOL_TAIL:top, :]

    u = z[:, d_pool:]
    ub = u.astype(BF16)
    up = jnp.concatenate([_dot(perm_ref[...], ub[h * SCAN_TILE:(h + 1) * SCAN_TILE]) for h in range(n_sub)],
                         axis=0)
    upb = up.astype(BF16)
    for m in range(n_slab):
        r = _dot(upb[:, m * LANES:(m + 1) * LANES], wb_ref[m])
        bur[:, m * STATE_SLAB:(m + 1) * STATE_SLAB] = r[:, :STATE_SLAB]
        bui[:, m * STATE_SLAB:(m + 1) * STATE_SLAB] = r[:, STATE_SLAB:]

    def scan_all(sending):
        row = lax.broadcasted_iota(I32, (SUBLANES, SCAN_LANES), 0)
        for h in range(n_sub):
            for c in range(n_chunk):
                scan_chunk(h, c, row, sending)

    def scan_chunk(h, c, row, sending):
        cs = slice(c * SCAN_LANES, (c + 1) * SCAN_LANES)
        row0 = h * SCAN_TILE
        loop0 = (h * n_chunk + c) * 2 * seg
        ar = are_ref[:, cs]
        ai = aim_ref[:, cs]

        def scan_body(s, carry):
            xr, xi = carry
            r0 = pl.multiple_of(row0 + s * SUBLANES, SUBLANES)
            nxr = ar * xr - ai * xi + bur[pl.ds(r0, SUBLANES), cs]
            nxi = ar * xi + ai * xr + bui[pl.ds(r0, SUBLANES), cs]
            bur[pl.ds(r0, SUBLANES), cs] = nxr
            bui[pl.ds(r0, SUBLANES), cs] = nxi
            if sending:
                send_some((loop0 + s) * per_iter)
            return nxr, nxi

        zero = jnp.zeros((SUBLANES, SCAN_LANES), F32)
        er, ei = lax.fori_loop(0, seg, scan_body, (zero, zero), unroll=True if sending else 4)

        fr = jnp.where(row == 0, st_r[:, cs], pltpu.roll(er, 1, 0))
        fi = jnp.where(row == 0, st_i[:, cs], pltpu.roll(ei, 1, 0))
        for ki, k in enumerate((1, 2, 4)):
            mr = mkre_ref[ki, :, cs]
            mi = mkim_ref[ki, :, cs]
            rr = pltpu.roll(fr, k, 0)
            ri = pltpu.roll(fi, k, 0)
            fr, fi = fr + mr * rr - mi * ri, fi + mr * ri + mi * rr
        asr = asre_ref[:, cs]
        asi = asim_ref[:, cs]
        nsr = asr * fr - asi * fi + er
        nsi = asr * fi + asi * fr + ei
        st_r[:, cs] = pltpu.roll(nsr, 1, 0)
        st_i[:, cs] = pltpu.roll(nsi, 1, 0)

        def fix_body(s, carry):
            t0 = pl.multiple_of(s * SUBLANES, SUBLANES)
            r0 = pl.multiple_of(row0 + s * SUBLANES, SUBLANES)
            pr = apre_ref[pl.ds(t0, SUBLANES), cs]
            pi = apim_ref[pl.ds(t0, SUBLANES), cs]
            bur[pl.ds(r0, SUBLANES), cs] = bur[pl.ds(r0, SUBLANES), cs] + pr * fr - pi * fi
            bui[pl.ds(r0, SUBLANES), cs] = bui[pl.ds(r0, SUBLANES), cs] + pr * fi + pi * fr
            if sending:
                send_some((loop0 + seg + s) * per_iter)
            return carry

        lax.fori_loop(0, seg, fix_body, 0, unroll=True if sending else 4)

    scan_all(True)

    ys = []
    for m in range(n_slab):
        cs = slice(m * STATE_SLAB, (m + 1) * STATE_SLAB)
        ys.append(_dot(bur[:, cs].astype(BF16), wcre_ref[m]) + _dot(bui[:, cs].astype(BF16), wcim_ref[m]))
    y = jnp.concatenate(ys, axis=-1) + d_ref[...] * up
    y = jax.nn.gelu(y)
    y = y * jax.nn.sigmoid(_dot(y.astype(BF16), gw_ref[...]) + gb_ref[...])
    yb = y.astype(BF16)
    y_ssm = jnp.concatenate([_dot(permt_ref[...], yb[h * SCAN_TILE:(h + 1) * SCAN_TILE]) for h in range(n_sub)],
                            axis=0)

    mix = _dot(y_pool.astype(BF16), wout_ref[0:d_pool, :]) + _dot(y_ssm.astype(BF16), wout_ref[d_pool:, :])
    x1 = x + mix

    hb = _rms(x1, nf_ref[...]).astype(BF16)
    logits = _dot(hb, wr_ref[...]) + br_ref[...]
    lane = lax.broadcasted_iota(I32, (tc, ROUTER_LANES), 1)
    gi, lo, i0, i1 = _route(logits, lane)
    ea = jnp.minimum(i0, i1) - lo
    eb = jnp.maximum(i0, i1) - lo
    cls = gi * PAIRS_PER_GROUP + ((ea * (7 - ea)) >> 1) + (eb - ea - 1)
    oh = (lane == cls).astype(F32)
    ohb = oh.astype(BF16)
    before = _dot(tri_ref[...], ohb) * oh
    ones8 = jnp.ones((SUBLANES, ROUTER_LANES), BF16)
    before_hi = jnp.floor(before * (1.0 / BF16_EXACT))
    before_lo = before - before_hi * float(BF16_EXACT)
    rank = (_dot_nt(ones8, before_hi.astype(BF16))[0:1, :] * float(BF16_EXACT)
            + _dot_nt(ones8, before_lo.astype(BF16))[0:1, :])
    cnt = jnp.sum(oh, axis=0, keepdims=True)

    st = alloc_v[...]
    open_t, fill, free = st[0:1], st[1:2], st[2:3]
    room = float(TILE_ROWS) - fill
    spill = cnt > room
    spill_f = spill.astype(F32)
    upper = (lax.broadcasted_iota(I32, (LANES, LANES), 0) < lax.broadcasted_iota(I32, (LANES, LANES), 1))
    earlier = _dot(_rows8(spill_f).astype(BF16), upper.astype(F32).astype(BF16))[0:1]
    new_t = free + earlier
    fill_hi = jnp.floor(fill * (1.0 / BF16_EXACT))
    per_row = _dot_nt(_rows8(open_t, new_t, fill_hi, fill - fill_hi * float(BF16_EXACT)).astype(BF16), ohb)
    open_r, new_r = per_row[0:1], per_row[1:2]
    fill_r = per_row[2:3] * float(BF16_EXACT) + per_row[3:4]
    room_r = float(TILE_ROWS) - fill_r
    dst = jnp.where(rank < room_r, open_r * float(TILE_ROWS) + fill_r + rank,
                    new_r * float(TILE_ROWS) + rank - room_r).astype(I32)
    dst_ref[0] = dst
    alloc_ref[0] = jnp.where(spill, new_t, -1.0).astype(I32)
    new_state = _rows8(jnp.where(spill, new_t, open_t), jnp.where(spill, cnt - room, fill + cnt),
                       free + jnp.sum(spill_f, axis=-1, keepdims=True))
    alloc_v[...] = new_state
    fin_ref[...] = new_state.astype(I32)

    @pl.when(step >= 1)
    def _():
        sent(slot).wait()

    for c in range(SUBLANES):
        x1t[slot, pl.ds(c, tc, stride=SUBLANES), :] = x1[:, c * LANES:(c + 1) * LANES]
    meta_v[0:1, :] = dst
    meta_cp.start()

    @pl.when(step == n_steps - 1)
    def _():
        meta_cp.wait()
        send_rows(slot)
        sent(1 - slot).wait()


        zsrc = x1t.at[1 - slot]
        zsrc[...] = jnp.zeros((tc * SUBLANES, LANES), F32)
        meta_v[:, 0:LANES] = new_state.astype(I32)
        for k in range(3):
            cp = pltpu.make_async_copy(meta_v.at[k, pl.ds(0, LANES)], meta_s.at[pl.ds(k * LANES, LANES)], msem)
            cp.start()
            cp.wait()

        def zcopy(first_row, n_rows):
            return pltpu.make_async_copy(
                zsrc.at[pl.ds(0, n_rows * SUBLANES)],
                xs_hbm.at[pl.ds(pl.multiple_of(first_row * SUBLANES, SUBLANES), n_rows * SUBLANES)], zsem)

        for c in range(N_CLASSES):
            tile = meta_s[c]
            fill_c = meta_s[LANES + c]
            rem = TILE_ROWS - fill_c
            for bit in range(TILE_ROWS_LOG2, -1, -1):
                size = 1 << bit
                first = (tile << TILE_ROWS_LOG2) + fill_c + (rem & ~(2 * size - 1))

                @pl.when((rem & size) != 0)
                def _(first=first, size=size):
                    cp = zcopy(first, size)
                    cp.start()
                    cp.wait()

        def tail(k, carry):
            cp = zcopy(k << TILE_ROWS_LOG2, TILE_ROWS)
            cp.start()
            cp.wait()
            return carry

        lax.fori_loop(meta_s[2 * LANES], n_tiles, tail, 0)
        sent(slot).wait()


def _ssm_tables(a_re, a_im, log_step, b_re, b_im, c_re, c_im, seg):
    g, p = a_re.shape
    h = b_re.shape[-1]
    n_slab = g // SLAB_GROUPS
    lr, li = a_re.astype(F32), a_im.astype(F32)
    step = jnp.exp(log_step.astype(F32))[:, None]
    mag = jnp.exp(lr * step)
    ab_re = mag * jnp.cos(li * step)
    ab_im = mag * jnp.sin(li * step)
    den = lr * lr + li * li
    nr, ni = ab_re - 1.0, ab_im
    q_re = (nr * lr + ni * li) / den
    q_im = (ni * lr - nr * li) / den
    br, bi = b_re.astype(F32), b_im.astype(F32)
    bb_re = q_re[..., None] * br - q_im[..., None] * bi
    bb_im = q_re[..., None] * bi + q_im[..., None] * br

    eye = jnp.eye(SLAB_GROUPS, dtype=F32)

    def b_slab(bb):
        bb = bb.reshape(n_slab, SLAB_GROUPS, p, h)
        w = jnp.einsum('mgph,gk->mghkp', bb, eye)
        return w.reshape(n_slab, SLAB_GROUPS * h, SLAB_GROUPS * p)

    wb = jnp.concatenate([b_slab(bb_re), b_slab(bb_im)], axis=-1).astype(BF16)

    def c_slab(cc):
        cc = cc.astype(F32).reshape(n_slab, SLAB_GROUPS, h, p)
        w = jnp.einsum('mghp,gk->mgpkh', cc, eye)
        return w.reshape(n_slab, SLAB_GROUPS * p, SLAB_GROUPS * h)

    wc_re = c_slab(c_re).astype(BF16)
    wc_im = c_slab(-c_im).astype(BF16)

    def apow(nn):
        nn = nn.astype(F32)[:, None, None]
        m_ = jnp.exp(nn * (lr * step))
        ang = nn * (li * step)
        return (m_ * jnp.cos(ang)).reshape(-1, g * p), (m_ * jnp.sin(ang)).reshape(-1, g * p)

    a1r, a1i = apow(jnp.ones((1,), F32))
    a_r = jnp.broadcast_to(a1r, (SUBLANES, g * p))
    a_i = jnp.broadcast_to(a1i, (SUBLANES, g * p))
    pr, pi = apow(jnp.arange(1, seg + 1))
    ap_r = jnp.repeat(pr, SUBLANES, axis=0)
    ap_i = jnp.repeat(pi, SUBLANES, axis=0)
    ks = jnp.array([1, 2, 4])
    mr, mi = apow(ks * seg)
    rows = jnp.arange(SUBLANES)[None, :, None]
    mask = (rows >= ks[:, None, None]).astype(F32)
    mk_r = mr[:, None, :] * mask
    mk_i = mi[:, None, :] * mask
    sr, si = apow(jnp.full((1,), seg, F32))
    as_r = jnp.broadcast_to(sr, (SUBLANES, g * p))
    as_i = jnp.broadcast_to(si, (SUBLANES, g * p))
    return wb, a_r, a_i, ap_r, ap_i, mk_r, mk_i, as_r, as_i, wc_re, wc_im


def _router_weights(w_coarse, b_coarse, w_fine, b_fine):
    d = w_coarse.shape[0]
    pad = ROUTER_LANES - N_EXPERT_GROUPS - N_EXPERTS
    wr = jnp.concatenate([w_coarse, w_fine, jnp.zeros((d, pad), F32)], axis=-1).astype(BF16)
    br = jnp.concatenate([b_coarse, b_fine, jnp.zeros((pad,), F32)]).reshape(1, -1).astype(F32)
    return wr, br


def _mixer(x, norm_mix, w_in, pool_w, pool_scale, a_re, a_im, log_step, b_re, b_im, c_re, c_im,
           d_skip, glu_w, glu_b, w_out, norm_ffn, wr, br):
    b, l, d = x.shape
    d_pool = pool_scale.shape[-1]
    d_ssm = d_skip.shape[-1]
    tc = TIME_TILE
    assert l % tc == 0 and tc % SCAN_TILE == 0
    assert d_pool == len(POOL_WINDOWS) * POOL_GROUP and d_ssm % LANES == 0
    assert d == SUBLANES * LANES, "one token row must fill exactly one (8, 128) tile"
    assert tc == TILE_ROWS, "a mixer tile must never need more than one new sorted tile per class"
    assert N_CLASSES <= LANES
    seg = SCAN_TILE // SUBLANES
    n_state = a_re.shape[0] * a_re.shape[1]
    assert (b * l) // TILE_ROWS + N_CLASSES <= BF16_EXACT, "tile indices pass through bf16 operands"
    tables = _ssm_tables(a_re, a_im, log_step, b_re, b_im, c_re, c_im, seg)
    wb, a_r, a_i, ap_r, ap_i, mk_r, mk_i, as_r, as_i, wc_re, wc_im = tables

    i = jnp.arange(SCAN_TILE)
    src = (i % SUBLANES) * seg + i // SUBLANES
    perm = (src[:, None] == i[None, :]).astype(BF16)
    permt = perm.T
    it = jnp.arange(tc)
    tri = (it[None, :] < it[:, None]).astype(BF16)

    row2 = lambda a: a.reshape(1, -1).astype(F32)
    operands = [
        x, row2(norm_mix), w_in.astype(BF16), pool_w.astype(BF16), row2(pool_scale), perm, permt, wb,
        a_r, a_i, ap_r, ap_i, mk_r, mk_i, as_r, as_i, wc_re, wc_im,
        row2(d_skip), glu_w.astype(BF16), row2(glu_b), w_out.astype(BF16),
        row2(norm_ffn), wr, br, tri,
    ]

    def whole(a):
        nd = a.ndim
        return pl.BlockSpec(a.shape, lambda bi, ni, nd=nd: (0,) * nd)

    n_t = l // tc
    n_tiles = (b * l) // TILE_ROWS + N_CLASSES
    in_specs = [pl.BlockSpec((1, tc, d), lambda bi, ni: (bi, ni, 0))] + [whole(a) for a in operands[1:]]
    kern = functools.partial(_mixer_kernel, tc=tc, d_pool=d_pool, d_ssm=d_ssm, n_tiles=n_tiles)
    return pl.pallas_call(
        kern,
        grid=(b, n_t),
        in_specs=in_specs,
        out_specs=[
            pl.BlockSpec(memory_space=pl.ANY),
            pl.BlockSpec((1, 1, tc), lambda bi, ni: (bi * n_t + ni, 0, 0)),
            pl.BlockSpec((1, 1, LANES), lambda bi, ni: (bi * n_t + ni, 0, 0)),
            pl.BlockSpec((SUBLANES, LANES), lambda bi, ni: (0, 0)),
        ],
        out_shape=[
            jax.ShapeDtypeStruct((n_tiles * TILE_ROWS * SUBLANES, LANES), F32),
            jax.ShapeDtypeStruct((b * n_t, 1, tc), I32),
            jax.ShapeDtypeStruct((b * n_t, 1, LANES), I32),
            jax.ShapeDtypeStruct((SUBLANES, LANES), I32),
        ],
        scratch_shapes=[
            pltpu.VMEM((POOL_HIST + tc, d_pool), F32),
            pltpu.VMEM((POOL_HIST + tc, d_pool), F32),
            pltpu.VMEM((POOL_HIST + tc, d_pool - POOL_GROUP), F32),
            pltpu.VMEM((POOL_HIST + tc, d_pool - 2 * POOL_GROUP), F32),
            pltpu.VMEM((tc, n_state), F32),
            pltpu.VMEM((tc, n_state), F32),
            pltpu.VMEM((SUBLANES, n_state), F32),
            pltpu.VMEM((SUBLANES, n_state), F32),
            pltpu.VMEM((2, tc * SUBLANES, LANES), F32),
            pltpu.VMEM((SUBLANES, tc), I32),
            pltpu.SMEM((max(tc, 3 * LANES),), I32),
            pltpu.VMEM((SUBLANES, LANES), F32),
            pltpu.SemaphoreType.DMA((2,)),
            pltpu.SemaphoreType.DMA,
            pltpu.SemaphoreType.DMA,
        ],
        compiler_params=pltpu.CompilerParams(
            dimension_semantics=("arbitrary", "arbitrary"), vmem_limit_bytes=VMEM_LIMIT),
        name="mixer",
    )(*operands)


def _experts_kernel(alloc_ref, fin_ref, x_ref, nf_ref, wr_ref, br_ref, wg_hbm, wu_hbm, wd_hbm,
                    nfin_ref, o_ref, wg_ref, wu_ref, wd_ref, tcls, tnv, wsem, *, n_steps):
    i = pl.program_id(0)
    n_tiles = pl.num_programs(0)
    rows = EXPERT_ROWS

    @pl.when(i == 0)
    def _():
        copies = [pltpu.make_async_copy(src, dst, wsem.at[j])
                  for j, (src, dst) in enumerate(((wg_hbm, wg_ref), (wu_hbm, wu_ref), (wd_hbm, wd_ref)))]
        for cp in copies:
            cp.start()
        free = fin_ref[2 * LANES]

        def init(k, carry):
            tcls[k] = jnp.minimum(k, N_CLASSES - 1)
            tnv[k] = jnp.where(k < free, TILE_ROWS, 0)
            return carry

        lax.fori_loop(0, n_tiles, init, 0)

        def log(s, carry):
            for c in range(N_CLASSES):
                t = alloc_ref[s * LANES + c]
                tcls[jnp.where(t >= 0, t, n_tiles)] = c
            return carry

        lax.fori_loop(0, n_steps, log, 0)
        for c in range(N_CLASSES):
            tnv[fin_ref[c]] = fin_ref[LANES + c]
        for cp in copies:
            cp.wait()

    c = tcls[i]
    g = (c >= PAIRS_PER_GROUP).astype(I32) + (c >= 2 * PAIRS_PER_GROUP).astype(I32) \
        + (c >= 3 * PAIRS_PER_GROUP).astype(I32)
    p = c - g * PAIRS_PER_GROUP
    a = (p >= 3).astype(I32) + (p >= 5).astype(I32)
    b = p - ((a * (7 - a)) >> 1) + a + 1
    ea = g * EXPERTS_PER_GROUP + a
    eb = g * EXPERTS_PER_GROUP + b

    def part(h):
        first = h * rows * SUBLANES

        @pl.when(tnv[i] <= h * rows)
        def _():
            o_ref[pl.ds(first, rows * SUBLANES), :] = jnp.zeros((rows * SUBLANES, LANES), F32)

        @pl.when(tnv[i] > h * rows)
        def _():
            compute(first)

    def compute(first):
        x = jnp.concatenate([x_ref[pl.ds(first + k, rows, stride=SUBLANES), :] for k in range(SUBLANES)], axis=-1)
        hb = _rms(x, nf_ref[...]).astype(BF16)
        logits = _dot(hb, wr_ref[...]) + br_ref[...]
        lane = lax.broadcasted_iota(I32, (rows, ROUTER_LANES), 1)
        is_c = lane < N_EXPERT_GROUPS
        cmax = jnp.max(jnp.where(is_c, logits, NEG), axis=-1, keepdims=True)
        psum = jnp.sum(jnp.where(is_c, jnp.exp(logits - cmax), 0.0), axis=-1, keepdims=True)
        cg = jnp.sum(jnp.where(lane == g, logits, 0.0), axis=-1, keepdims=True)
        p_g = jnp.exp(cg - cmax) / psum
        va = jnp.sum(jnp.where(lane == N_EXPERT_GROUPS + ea, logits, 0.0), axis=-1, keepdims=True)
        vb = jnp.sum(jnp.where(lane == N_EXPERT_GROUPS + eb, logits, 0.0), axis=-1, keepdims=True)
        vmax = jnp.maximum(va, vb)
        exa = jnp.exp(va - vmax)
        exb = jnp.exp(vb - vmax)
        den = exa + exb
        y = jnp.zeros_like(x)
        for e, w in ((ea, p_g * (exa / den)), (eb, p_g * (exb / den))):
            act = jax.nn.silu(_dot(hb, wg_ref[e])) * _dot(hb, wu_ref[e])
            y = y + _dot((act * w).astype(BF16), wd_ref[e])
        out = _rms(x + y, nfin_ref[...])
        for k in range(SUBLANES):
            o_ref[pl.ds(first + k, rows, stride=SUBLANES), :] = out[:, k * LANES:(k + 1) * LANES]

    for h in range(TILE_ROWS // rows):
        part(h)


def _experts(xs, alloc, fin, norm_ffn, wr, br, w_gate, w_up, w_down, norm_final):
    n_e, d, f = w_gate.shape
    assert n_e == N_EXPERTS
    blk = TILE_ROWS * SUBLANES
    n_tiles = xs.shape[0] // blk
    n_steps = alloc.shape[0] // LANES
    row2 = lambda a: a.reshape(1, -1).astype(F32)
    const = lambda shape: pl.BlockSpec(shape, lambda i, al, fi: (0,) * len(shape))
    anyspec = pl.BlockSpec(memory_space=pl.ANY)
    grid_spec = pltpu.PrefetchScalarGridSpec(
        num_scalar_prefetch=2,
        grid=(n_tiles,),
        in_specs=[
            pl.BlockSpec((blk, LANES), lambda i, al, fi: (i, 0)),
            const((1, d)), const((d, ROUTER_LANES)), const((1, ROUTER_LANES)),
            anyspec, anyspec, anyspec,
            const((1, d)),
        ],
        out_specs=pl.BlockSpec((blk, LANES), lambda i, al, fi: (i, 0)),
        scratch_shapes=[
            pltpu.VMEM((n_e, d, f), BF16),
            pltpu.VMEM((n_e, d, f), BF16),
            pltpu.VMEM((n_e, f, d), BF16),
            pltpu.SMEM((n_tiles + 1,), I32),
            pltpu.SMEM((n_tiles + 1,), I32),
            pltpu.SemaphoreType.DMA((3,)),
        ],
    )
    return pl.pallas_call(
        functools.partial(_experts_kernel, n_steps=n_steps),
        grid_spec=grid_spec,
        out_shape=jax.ShapeDtypeStruct(xs.shape, F32),
        compiler_params=pltpu.CompilerParams(
            dimension_semantics=("arbitrary",), vmem_limit_bytes=VMEM_LIMIT),
        name="experts",
    )(alloc, fin, xs, row2(norm_ffn), wr, br,
      w_gate.astype(BF16), w_up.astype(BF16), w_down.astype(BF16), row2(norm_final))


def _unsort_kernel(dst_hbm, ys_hbm, o_ref, idx, buf, isem, gsem):
    i = pl.program_id(0)
    nt = pl.num_programs(0)
    slot = lax.rem(i, 2)
    rows = o_ref.shape[0]

    def idx_copy(tile, s3):
        return pltpu.make_async_copy(dst_hbm.at[tile], idx.at[pl.ds(s3 * rows, rows)], isem.at[s3])

    def gather_start(s3, bslot):
        def body(r2, c):
            for prio in range(2):
                r = r2 * 2 + prio
                src = idx[s3 * rows + r]
                pltpu.make_async_copy(ys_hbm.at[pl.ds(pl.multiple_of(src * SUBLANES, SUBLANES), SUBLANES)],
                                      buf.at[bslot, pl.ds(pl.multiple_of(r * SUBLANES, SUBLANES), SUBLANES)],
                                      gsem.at[bslot]).start(priority=prio)
            return c

        lax.fori_loop(0, rows // 2, body, 0, unroll=4)

    @pl.when(i == 0)
    def _():
        idx_copy(0, 0).start()
        idx_copy(0, 0).wait()

        @pl.when(nt > 1)
        def _():
            idx_copy(1, 1).start()
            idx_copy(1, 1).wait()

        gather_start(0, 0)

    @pl.when(i + 2 < nt)
    def _():
        idx_copy(i + 2, lax.rem(i + 2, 3)).start()

    @pl.when((i >= 1) & (i + 1 < nt))
    def _():
        idx_copy(i + 1, lax.rem(i + 1, 3)).wait()

    @pl.when(i + 1 < nt)
    def _():
        gather_start(lax.rem(i + 1, 3), 1 - slot)

    pltpu.make_async_copy(ys_hbm.at[pl.ds(0, rows * SUBLANES)], buf.at[slot], gsem.at[slot]).wait()
    o_ref[...] = jnp.concatenate(
        [buf[slot, pl.ds(c, rows, stride=SUBLANES), :] for c in range(SUBLANES)], axis=-1)


def _unsort(ys, dst, n_tok, d):
    n_tiles, _, rows = dst.shape
    anyspec = pl.BlockSpec(memory_space=pl.ANY)
    return pl.pallas_call(
        _unsort_kernel,
        grid=(n_tiles,),
        in_specs=[anyspec, anyspec],
        out_specs=pl.BlockSpec((rows, d), lambda i: (i, 0)),
        out_shape=jax.ShapeDtypeStruct((n_tok, d), F32),
        scratch_shapes=[
            pltpu.SMEM((3 * rows,), I32),
            pltpu.VMEM((2, rows * SUBLANES, LANES), F32),
            pltpu.SemaphoreType.DMA((3,)),
            pltpu.SemaphoreType.DMA((2,)),
        ],
        compiler_params=pltpu.CompilerParams(
            dimension_semantics=("arbitrary",), vmem_limit_bytes=VMEM_LIMIT),
        name="unsort",
    )(dst.reshape(n_tiles, rows), ys)


def kernel(x, norm_mix, w_in, pool_w, pool_scale, ssm_a_re, ssm_a_im, ssm_log_step, ssm_b_re, ssm_b_im, ssm_c_re, ssm_c_im, ssm_d, glu_w, glu_b, w_out, norm_ffn, router_coarse_w, router_coarse_b, router_fine_w, router_fine_b, exp_w_gate, exp_w_up, exp_w_down, norm_final):
    assert norm_mix.shape[0] == 1, "the experts call fuses the final norm: single-layer blocks only"
    b, l, d = x.shape
    wr, br = _router_weights(router_coarse_w[0], router_coarse_b[0], router_fine_w[0], router_fine_b[0])
    xs, dst, alloc, fin = _mixer(
        x, norm_mix[0], w_in[0], pool_w[0], pool_scale[0], ssm_a_re[0], ssm_a_im[0], ssm_log_step[0],
        ssm_b_re[0], ssm_b_im[0], ssm_c_re[0], ssm_c_im[0], ssm_d[0], glu_w[0], glu_b[0], w_out[0],
        norm_ffn[0], wr, br)
    ys = _experts(xs, alloc.reshape(-1), fin.reshape(-1), norm_ffn[0], wr, br,
                  exp_w_gate[0], exp_w_up[0], exp_w_down[0], norm_final)
    y = _unsort(ys, dst, b * l, d)
    return y.reshape(b, l, d)
```
